```python
import math
import jax
import jax.numpy as jnp
from jax import lax
import numpy as np

D_MODEL = 1024
BATCH = 8
SEQ = 2048
DEPTH = 2
DEC_BATCH = 128
DEC_SEQ = 8
PAST_LEN = 16384
PAGE_SIZE = 128

N_META = 16
POOL_GROUPS = 4
POOL_GROUP_W = D_MODEL // 8
POOL_WIDTH = POOL_GROUPS * POOL_GROUP_W
POOL_WINDOWS = (2, 4, 8, 16)
POOL_STATE = 15
HEAD_K = 128
HEAD_V = 128
N_HEADS = D_MODEL // HEAD_K
KEY_W = N_HEADS * HEAD_K
VAL_W = N_HEADS * HEAD_V
QKV_W = 2 * KEY_W + VAL_W
CONV_W = 4
CHUNK = 64
D_FF = 4 * D_MODEL
N_BRANCH = 2
IN_SPLITS = (POOL_WIDTH, POOL_WIDTH + QKV_W, POOL_WIDTH + QKV_W + N_HEADS, POOL_WIDTH + QKV_W + 2 * N_HEADS, POOL_WIDTH + QKV_W + 2 * N_HEADS + VAL_W)
IN_W = POOL_WIDTH + QKV_W + 2 * N_HEADS + VAL_W + N_BRANCH * D_MODEL
EPS = 1e-6

kernel_name = 'pool_gated_deltanet_hybrid_step'


def rms_norm(x, g):
    xf = x.astype(jnp.float32)
    y = xf * lax.rsqrt(jnp.mean(xf * xf, axis=-1, keepdims=True) + EPS)
    return (y * g.astype(jnp.float32)).astype(x.dtype)


def l2_normalize(x):
    return x * lax.rsqrt(jnp.sum(x * x, axis=-1, keepdims=True) + EPS)


def causal_dwconv(x_ext, w):
    return lax.conv_general_dilated(x_ext, w.astype(x_ext.dtype)[:, None, :], window_strides=(1,), padding='VALID', dimension_numbers=('NWC', 'WIO', 'NWC'), feature_group_count=x_ext.shape[-1])


def pool_mix(ext, pos0, pool_w, pool_scale):
    bsz, lx, _ = ext.shape
    t_len = lx - POOL_STATE
    ef = ext.astype(jnp.float32)
    cs = jnp.concatenate([jnp.zeros((bsz, 1, POOL_WIDTH), jnp.float32), jnp.cumsum(ef, axis=1)], axis=1)
    pos_count = jnp.arange(1, t_len + 1) + pos0
    means = []
    for gi, w in enumerate(POOL_WINDOWS):
        lo, hi = gi * POOL_GROUP_W, (gi + 1) * POOL_GROUP_W
        s = cs[:, POOL_STATE + 1:POOL_STATE + 1 + t_len, lo:hi] - cs[:, POOL_STATE + 1 - w:POOL_STATE + 1 - w + t_len, lo:hi]
        cnt = jnp.minimum(pos_count, w).astype(jnp.float32)
        means.append(s / cnt[None, :, None])
    mean = jnp.stack(means, axis=2)
    tok = ef[:, POOL_STATE:].reshape(bsz, t_len, POOL_GROUPS, POOL_GROUP_W)
    y = jnp.einsum('btgc,gcd->btgd', mean - tok, pool_w.astype(jnp.float32)) * pool_scale.astype(jnp.float32).reshape(POOL_GROUPS, POOL_GROUP_W)
    return y.reshape(bsz, t_len, POOL_WIDTH).astype(ext.dtype)


def gdn_chunked(q, k, v, g, beta, s0, chunk):
    bsz, t_len, nh, dk = q.shape
    dv = v.shape[-1]
    n = -(-t_len // chunk)
    pad = n * chunk - t_len

    def blk(a):
        a = jnp.pad(a, [(0, 0), (0, pad)] + [(0, 0)] * (a.ndim - 2))
        a = a.reshape((bsz, n, chunk) + a.shape[2:])
        return jnp.moveaxis(a, (1, 3), (0, 2))

    qc, kc, vc, gc, bc = blk(q), blk(k), blk(v), blk(g), blk(beta)
    gc = jnp.cumsum(gc, axis=-1)
    tril = jnp.tril(jnp.ones((chunk, chunk), bool))
    strict = jnp.tril(jnp.ones((chunk, chunk), bool), -1)
    diff = gc[..., :, None] - gc[..., None, :]
    decay = jnp.where(tril, jnp.exp(jnp.where(tril, diff, 0.0)), 0.0)
    kb = kc * bc[..., None]
    lmat = jnp.where(strict, jnp.einsum('nbhik,nbhjk->nbhij', kb, kc) * decay, 0.0)
    amat = lmat + jnp.eye(chunk, dtype=jnp.float32)
    rhs = jnp.concatenate([vc * bc[..., None], kb * jnp.exp(gc)[..., None]], axis=-1)
    sol = lax.linalg.triangular_solve(amat, rhs, left_side=True, lower=True, unit_diagonal=True)
    u_c, w_c = sol[..., :dv], sol[..., dv:]

    def step(s, xs):
        q_i, k_i, u_i, w_i, g_i, d_i = xs
        attn = jnp.einsum('bhik,bhjk->bhij', q_i, k_i) * d_i
        v_new = u_i - jnp.einsum('bhck,bhkv->bhcv', w_i, s)
        o = jnp.einsum('bhck,bhkv->bhcv', q_i * jnp.exp(g_i)[..., None], s) + jnp.einsum('bhij,bhjv->bhiv', attn, v_new)
        g_last = g_i[..., -1:]
        s = s * jnp.exp(g_last)[..., None] + jnp.einsum('bhck,bhcv->bhkv', k_i * jnp.exp(g_last - g_i)[..., None], v_new)
        return s, o

    s_fin, o = lax.scan(step, s0, (qc, kc, u_c, w_c, gc, decay))
    o = jnp.moveaxis(o, (0, 2), (1, 3)).reshape(bsz, n * chunk, nh, dv)[:, :t_len]
    return o, s_fin


def delta_mixer(qkv_ext, b, a, z, s0, n_meta, conv_w, a_log, dt_bias, o_norm_g):
    bsz, t_len = b.shape[0], b.shape[1]
    c = jax.nn.silu(causal_dwconv(qkv_ext, conv_w).astype(jnp.float32))
    q = l2_normalize(c[..., :KEY_W].reshape(bsz, t_len, N_HEADS, HEAD_K)) * (HEAD_K ** -0.5)
    k = l2_normalize(c[..., KEY_W:2 * KEY_W].reshape(bsz, t_len, N_HEADS, HEAD_K))
    v = c[..., 2 * KEY_W:].reshape(bsz, t_len, N_HEADS, HEAD_V)
    beta = jax.nn.sigmoid(b.astype(jnp.float32))
    g = -jnp.exp(a_log.astype(jnp.float32)) * jax.nn.softplus(a.astype(jnp.float32) + dt_bias.astype(jnp.float32))
    if n_meta > 0:
        o_m, s_m = gdn_chunked(q[:, :n_meta], k[:, :n_meta], v[:, :n_meta], g[:, :n_meta], beta[:, :n_meta], s0, n_meta)
        o_r, s_new = gdn_chunked(q[:, n_meta:], k[:, n_meta:], v[:, n_meta:], g[:, n_meta:], beta[:, n_meta:], s_m, CHUNK)
        o = jnp.concatenate([o_m, o_r], axis=1)
    else:
        o, s_new = gdn_chunked(q, k, v, g, beta, s0, min(CHUNK, t_len))
    o = o * lax.rsqrt(jnp.mean(o * o, axis=-1, keepdims=True) + EPS) * o_norm_g.astype(jnp.float32)
    o = o * jax.nn.silu(z.astype(jnp.float32).reshape(bsz, t_len, N_HEADS, HEAD_V))
    return o.reshape(bsz, t_len, VAL_W).astype(z.dtype), s_new


def trunk_layer(x, conv_prefix, s0, pool_prefix, pos0, n_meta, lp):
    dt = x.dtype
    h = rms_norm(x, lp['g_pre_mix'])
    proj = h @ lp['w_in'].astype(dt)
    u, qkv, b, a, z, gates = jnp.split(proj, list(IN_SPLITS), axis=-1)
    pool_ext = jnp.concatenate([pool_prefix.astype(dt), u], axis=1)
    a_out = pool_mix(pool_ext, pos0, lp['pool_w'], lp['pool_scale'])
    qkv_ext = jnp.concatenate([conv_prefix.astype(dt), qkv], axis=1)
    d_out, s_new = delta_mixer(qkv_ext, b, a, z, s0, n_meta, lp['conv_w'], lp['a_log'], lp['dt_bias'], lp['o_norm_g'])
    br_pool = (a_out @ lp['w_branch_pool'].astype(dt)).astype(jnp.float32)
    br_delta = (d_out @ lp['w_branch_delta'].astype(dt)).astype(jnp.float32)
    gate_pool, gate_delta = jnp.split(jax.nn.sigmoid(gates.astype(jnp.float32)), N_BRANCH, axis=-1)
    m = (gate_pool * br_pool + gate_delta * br_delta).astype(dt) @ lp['w_out'].astype(dt)
    x = x + rms_norm(m, lp['g_post_mix'])
    h2 = rms_norm(x, lp['g_pre_ffn'])
    f = jnp.square(jax.nn.relu(h2 @ lp['w_up'].astype(dt))) @ lp['w_down'].astype(dt)
    x = x + rms_norm(f, lp['g_post_ffn'])
    return x, qkv_ext[:, -(CONV_W - 1):], s_new.astype(dt), pool_ext[:, -POOL_STATE:]


def setup_inputs(seed: int = 0) -> dict:
    key = jax.random.key(seed)
    ks = jax.random.split(key, 24)
    nrm = jax.random.normal
    f32 = jnp.float32
    dt_init = jnp.exp(jax.random.uniform(ks[10], (DEPTH, N_HEADS), f32, math.log(1e-3), math.log(0.1)))
    return {
        'x_prompt': nrm(ks[0], (BATCH, SEQ, D_MODEL), f32),
        'x_sample': nrm(ks[1], (DEC_BATCH, DEC_SEQ, D_MODEL), f32),
        'state_conv': nrm(ks[2], (DEPTH, DEC_BATCH, CONV_W - 1, QKV_W), f32),
        'state_ssm': nrm(ks[3], (DEPTH, DEC_BATCH, N_HEADS, HEAD_K, HEAD_V), f32) * HEAD_K ** -0.5,
        'state_pool': nrm(ks[4], (DEPTH, DEC_BATCH, POOL_STATE, POOL_WIDTH), f32),
        'meta_tokens': nrm(ks[5], (N_META, D_MODEL), f32),
        'g_pre_mix': 1.0 + 0.02 * nrm(ks[6], (DEPTH, D_MODEL), f32),
        'w_in': nrm(ks[7], (DEPTH, D_MODEL, IN_W), f32) * D_MODEL ** -0.5,
        'conv_w': nrm(ks[8], (DEPTH, CONV_W, QKV_W), f32) * CONV_W ** -0.5,
        'a_log': jnp.log(jax.random.uniform(ks[9], (DEPTH, N_HEADS), f32, 1.0, 16.0)),
        'dt_bias': dt_init + jnp.log(-jnp.expm1(-dt_init)),
        'o_norm_g': 1.0 + 0.02 * nrm(ks[11], (DEPTH, HEAD_V), f32),
        'pool_w': nrm(ks[12], (DEPTH, POOL_GROUPS, POOL_GROUP_W, POOL_GROUP_W), f32) * POOL_GROUP_W ** -0.5,
        'pool_scale': 1.0 + 0.1 * nrm(ks[13], (DEPTH, POOL_WIDTH), f32),
        'w_branch_pool': nrm(ks[14], (DEPTH, POOL_WIDTH, D_MODEL), f32) * POOL_WIDTH ** -0.5,
        'w_branch_delta': nrm(ks[15], (DEPTH, VAL_W, D_MODEL), f32) * VAL_W ** -0.5,
        'w_out': nrm(ks[16], (DEPTH, D_MODEL, D_MODEL), f32) * D_MODEL ** -0.5,
        'g_post_mix': 1.0 + 0.02 * nrm(ks[17], (DEPTH, D_MODEL), f32),
        'g_pre_ffn': 1.0 + 0.02 * nrm(ks[18], (DEPTH, D_MODEL), f32),
        'w_up': nrm(ks[19], (DEPTH, D_MODEL, D_FF), f32) * D_MODEL ** -0.5,
        'w_down': nrm(ks[20], (DEPTH, D_FF, D_MODEL), f32) * D_FF ** -0.5,
        'g_post_ffn': 1.0 + 0.02 * nrm(ks[21], (DEPTH, D_MODEL), f32),
    }


def reference(x_prompt, x_sample, state_conv, state_ssm, state_pool, meta_tokens, g_pre_mix, w_in, conv_w, a_log, dt_bias, o_norm_g, pool_w, pool_scale, w_branch_pool, w_branch_delta, w_out, g_post_mix, g_pre_ffn, w_up, w_down, g_post_ffn):
    dt = x_prompt.dtype
    bp = x_prompt.shape[0]
    xp = jnp.concatenate([jnp.broadcast_to(meta_tokens.astype(dt)[None], (bp, N_META, D_MODEL)), x_prompt], axis=1)
    xs = x_sample
    zero_conv = jnp.zeros((bp, CONV_W - 1, QKV_W), dt)
    zero_pool = jnp.zeros((bp, POOL_STATE, POOL_WIDTH), dt)
    zero_ssm = jnp.zeros((bp, N_HEADS, HEAD_K, HEAD_V), jnp.float32)
    conv_p, ssm_p, pool_p, conv_s, ssm_s, pool_s = [], [], [], [], [], []
    for l in range(DEPTH):
        lp = {'g_pre_mix': g_pre_mix[l], 'w_in': w_in[l], 'conv_w': conv_w[l], 'a_log': a_log[l], 'dt_bias': dt_bias[l], 'o_norm_g': o_norm_g[l], 'pool_w': pool_w[l], 'pool_scale': pool_scale[l], 'w_branch_pool': w_branch_pool[l], 'w_branch_delta': w_branch_delta[l], 'w_out': w_out[l], 'g_post_mix': g_post_mix[l], 'g_pre_ffn': g_pre_ffn[l], 'w_up': w_up[l], 'w_down': w_down[l], 'g_post_ffn': g_post_ffn[l]}
        xp, c_p, s_p, p_p = trunk_layer(xp, zero_conv, zero_ssm, zero_pool, 0, N_META, lp)
        xs, c_s, s_s, p_s = trunk_layer(xs, state_conv[l], state_ssm[l].astype(jnp.float32), state_pool[l], PAST_LEN, 0, lp)
        conv_p.append(c_p)
        ssm_p.append(s_p)
        pool_p.append(p_p)
        conv_s.append(c_s)
        ssm_s.append(s_s)
        pool_s.append(p_s)
    y_prompt = xp[:, N_META:]
    return (y_prompt, xs, jnp.stack(conv_p), jnp.stack(ssm_p), jnp.stack(pool_p), jnp.stack(conv_s), jnp.stack(ssm_s), jnp.stack(pool_s))
```

```python
import functools

import jax
import jax.numpy as jnp
from jax import lax
from jax.experimental import pallas as pl
from jax.experimental.pallas import tpu as pltpu

D_MODEL = 1024
N_META = 16
POOL_GROUPS = 4
POOL_GROUP_W = 128
POOL_WIDTH = 512
POOL_WINDOWS = (2, 4, 8, 16)
POOL_STATE = 15
HEAD_K = 128
HEAD_V = 128
N_HEADS = 8
KEY_W = 1024
VAL_W = 1024
QKV_W = 3072
CONV_W = 4
D_FF = 4096
PAST_LEN = 16384
EPS = 1e-6

LANES = 128
SUBLANES = 8
SLAB = 128
GB_W = 128
VMEM_LIMIT = 56 * 1024 * 1024

F32 = jnp.float32
BF16 = jnp.bfloat16
HI = lax.Precision.HIGHEST


def _dot(a, b):
    return jnp.dot(a.astype(BF16), b.astype(BF16), preferred_element_type=F32)


def _dot_nt(a, b):
    return lax.dot_general(a.astype(BF16), b.astype(BF16), (((1,), (1,)), ((), ())),
                           preferred_element_type=F32)


def _dot_tn(a, b):
    return lax.dot_general(a.astype(BF16), b.astype(BF16), (((0,), (0,)), ((), ())),
                           preferred_element_type=F32)


def _dot_hi(a, b):
    return jnp.dot(a, b, precision=HI, preferred_element_type=F32)


def _dot_nt_hi(a, b):
    return lax.dot_general(a, b, (((1,), (1,)), ((), ())), precision=HI,
                           preferred_element_type=F32)


def _rms(x, g):
    return x * lax.rsqrt(jnp.mean(x * x, axis=-1, keepdims=True) + EPS) * g


def _sigmoid(x):
    return 1.0 / (1.0 + jnp.exp(-x))


def _silu(x):
    return x * _sigmoid(x)


def _softplus(x):
    return jnp.maximum(x, 0.0) + jnp.log1p(jnp.exp(-jnp.abs(x)))


def _in_proj_kernel(x_ref, g_ref, wu_ref, wqkv_ref, wz_ref, wg_ref, wba_ref, wbat_ref,
                    pcol_ref, prow_ref,
                    u_ref, qkv_ref, z_ref, gates_ref, gbc_ref, gbr_ref):
    h = _rms(x_ref[...], g_ref[...]).astype(BF16)
    u_ref[...] = jnp.dot(h, wu_ref[...], preferred_element_type=F32)
    qkv_ref[...] = jnp.dot(h, wqkv_ref[...], preferred_element_type=F32)
    z_ref[...] = jnp.dot(h, wz_ref[...], preferred_element_type=F32)
    gates_ref[...] = jnp.dot(h, wg_ref[...], preferred_element_type=F32)
    ba = jnp.dot(h, wba_ref[...], preferred_element_type=F32)
    lane = lax.broadcasted_iota(jnp.int32, ba.shape, 1)
    neg_a = -jnp.exp(pcol_ref[0:1, :])
    g_col = neg_a * _softplus(ba + pcol_ref[1:2, :])
    gbc_ref[...] = jnp.where(lane < N_HEADS, _sigmoid(ba), jnp.where(lane < 2 * N_HEADS, g_col, 0.0))
    bat = lax.dot_general(wbat_ref[...], h, (((1,), (1,)), ((), ())),
                          preferred_element_type=F32)
    row = lax.broadcasted_iota(jnp.int32, bat.shape, 0)
    neg_a_r = -jnp.exp(prow_ref[:, 0:1])
    g_row = neg_a_r * _softplus(bat + prow_ref[:, 1:2])
    gbr_ref[...] = jnp.where(row < N_HEADS, _sigmoid(bat), g_row)


def _in_proj(x, g, w, tm):
    n = x.shape[0]
    assert n % tm == 0
    const = lambda i: (0, 0)
    tok = lambda i: (i, 0)
    in_specs = [
        pl.BlockSpec((tm, D_MODEL), tok),
        pl.BlockSpec((1, D_MODEL), const),
        pl.BlockSpec((D_MODEL, POOL_WIDTH), const),
        pl.BlockSpec((D_MODEL, QKV_W), const),
        pl.BlockSpec((D_MODEL, VAL_W), const),
        pl.BlockSpec((D_MODEL, 2 * D_MODEL), const),
        pl.BlockSpec((D_MODEL, GB_W), const),
        pl.BlockSpec((2 * N_HEADS, D_MODEL), const),
        pl.BlockSpec((SUBLANES, GB_W), const),
        pl.BlockSpec((2 * N_HEADS, LANES), const),
    ]
    out_specs = [
        pl.BlockSpec((tm, POOL_WIDTH), tok),
        pl.BlockSpec((tm, QKV_W), tok),
        pl.BlockSpec((tm, VAL_W), tok),
        pl.BlockSpec((tm, 2 * D_MODEL), tok),
        pl.BlockSpec((tm, GB_W), tok),
        pl.BlockSpec((2 * N_HEADS, tm), lambda i: (0, i)),
    ]
    out_shape = [
        jax.ShapeDtypeStruct((n, POOL_WIDTH), F32),
        jax.ShapeDtypeStruct((n, QKV_W), F32),
        jax.ShapeDtypeStruct((n, VAL_W), F32),
        jax.ShapeDtypeStruct((n, 2 * D_MODEL), F32),
        jax.ShapeDtypeStruct((n, GB_W), F32),
        jax.ShapeDtypeStruct((2 * N_HEADS, n), F32),
    ]
    return pl.pallas_call(
        _in_proj_kernel,
        grid=(n // tm,),
        in_specs=in_specs,
        out_specs=out_specs,
        out_shape=out_shape,
        compiler_params=pltpu.CompilerParams(
            dimension_semantics=("parallel",), vmem_limit_bytes=VMEM_LIMIT),
        name="in_proj",
    )(x, g, w["wu"], w["wqkv"], w["wz"], w["wg"], w["wba"], w["wbat"], w["pcol"], w["prow"])


def _unit_lower_inverse(lmat, span):
    n = lmat.shape[0]
    eye = (lax.broadcasted_iota(jnp.int32, (n, n), 0)
           == lax.broadcasted_iota(jnp.int32, (n, n), 1)).astype(F32)
    inv = eye - lmat
    power = lmat
    covered = 2
    while covered < span:
        power = _dot_hi(power, power)
        inv = inv + _dot_hi(inv, power)
        covered *= 2
    return inv


def _mixer_kernel(nseq, seq_len, n_valid, pos0, n_steps,
                  u_ref, qkv_ref, z_ref, gbc_ref, gbr_ref, convp_ref, poolp_ref, s0_ref,
                  convw_ref, onorm_ref, poolw_ref, pscale_ref,
                  a_ref, d_ref, convo_ref, poolo_ref, so_ref,
                  qkv_ext, u_ext):
    t = pl.program_id(1)
    conv_base = SUBLANES
    pool_base = 2 * SUBLANES

    if n_steps > 1:
        @pl.when(t > 0)
        def _():
            qkv_ext[0, conv_base - (CONV_W - 1):conv_base, :] = (
                qkv_ext[0, conv_base + seq_len - (CONV_W - 1):conv_base + seq_len, :])
            u_ext[0, pool_base - POOL_STATE:pool_base, :] = (
                u_ext[0, pool_base + seq_len - POOL_STATE:pool_base + seq_len, :])

    @pl.when(t == 0)
    def _():
        for s in range(nseq):
            qkv_ext[s, conv_base - (CONV_W - 1):conv_base, :] = convp_ref[s]
            u_ext[s, pool_base - POOL_STATE:pool_base, :] = poolp_ref[s]
        so_ref[...] = s0_ref[...]

    for s in range(nseq):
        qkv_ext[s, conv_base:conv_base + seq_len, :] = qkv_ref[s * seq_len:(s + 1) * seq_len, :]
        u_ext[s, pool_base:pool_base + seq_len, :] = u_ref[s * seq_len:(s + 1) * seq_len, :]

    last = n_valid if n_valid < seq_len else seq_len
    for s in range(nseq):
        convo_ref[s] = qkv_ext[s, conv_base + last - (CONV_W - 1):conv_base + last, :]
        poolo_ref[s] = u_ext[s, pool_base + last - POOL_STATE:pool_base + last, :]

    for gi, win in enumerate(POOL_WINDOWS):
        lanes = slice(gi * POOL_GROUP_W, (gi + 1) * POOL_GROUP_W)
        if pos0 >= POOL_STATE:
            cnt = float(win)
        else:
            pos = lax.broadcasted_iota(jnp.int32, (seq_len, 1), 0) + (pos0 + 1) + t * seq_len
            cnt = jnp.minimum(pos, win).astype(F32)
        pieces = []
        for s in range(nseq):
            tok = u_ext[s, pool_base:pool_base + seq_len, lanes]
            tot = tok
            for j in range(1, win):
                tot = tot + u_ext[s, pool_base - j:pool_base - j + seq_len, lanes]
            pieces.append(tot / cnt - tok)
        dev = pieces[0] if nseq == 1 else jnp.concatenate(pieces, axis=0)
        a_ref[:, lanes] = _dot(dev, poolw_ref[gi]) * pscale_ref[:, lanes]

    gbc = gbc_ref[...]
    gbr = gbr_ref[...]
    if n_valid < seq_len:
        gbc = jnp.where(lax.broadcasted_iota(jnp.int32, gbc.shape, 0) < n_valid, gbc, 0.0)
        gbr = jnp.where(lax.broadcasted_iota(jnp.int32, gbr.shape, 1) < n_valid, gbr, 0.0)
    ri = lax.broadcasted_iota(jnp.int32, (SLAB, SLAB), 0)
    ci = lax.broadcasted_iota(jnp.int32, (SLAB, SLAB), 1)
    shift = seq_len.bit_length() - 1
    same = (ri >> shift) == (ci >> shift)
    m_le = same & (ci <= ri)
    m_lt = same & (ci < ri)
    gc_col = _dot_hi(m_le.astype(F32), gbc)
    gc_row = _dot_hi(gbr, (same & (ri <= ci)).astype(F32))
    gl_col = _dot_hi(same.astype(F32), gbc)

    for h in range(N_HEADS):
        ql = slice(h * HEAD_K, (h + 1) * HEAD_K)
        kl = slice(KEY_W + h * HEAD_K, KEY_W + (h + 1) * HEAD_K)
        vl = slice(2 * KEY_W + h * HEAD_V, 2 * KEY_W + (h + 1) * HEAD_V)

        def conv(lanes):
            pieces = []
            for s in range(nseq):
                acc = None
                for j in range(CONV_W):
                    lo = conv_base - (CONV_W - 1) + j
                    term = qkv_ext[s, lo:lo + seq_len, lanes] * convw_ref[j:j + 1, lanes]
                    acc = term if acc is None else acc + term
                pieces.append(acc)
            c = pieces[0] if nseq == 1 else jnp.concatenate(pieces, axis=0)
            return _silu(c)

        q = conv(ql)
        k = conv(kl)
        v = conv(vl)
        q = q * lax.rsqrt(jnp.sum(q * q, axis=-1, keepdims=True) + EPS) * (HEAD_K ** -0.5)
        k = k * lax.rsqrt(jnp.sum(k * k, axis=-1, keepdims=True) + EPS)

        beta = gbc[:, h:h + 1]
        g_c = gc_col[:, N_HEADS + h:N_HEADS + h + 1]
        g_r = gc_row[N_HEADS + h:N_HEADS + h + 1, :]
        g_l = gl_col[:, N_HEADS + h:N_HEADS + h + 1]
        decay = jnp.where(m_le, jnp.exp(jnp.where(m_le, g_c - g_r, 0.0)), 0.0)
        e_g = jnp.exp(g_c)
        kb = k * beta
        lmat = jnp.where(m_lt, _dot_nt_hi(kb, k) * decay, 0.0)
        attn = _dot_nt(q, k) * decay
        tinv = _unit_lower_inverse(lmat, seq_len)

        ks_parts, qs_parts = [], []
        for s in range(nseq):
            rows = slice(s * seq_len, (s + 1) * seq_len)
            kq = jnp.concatenate([k[rows], q[rows]], axis=0)
            prod = _dot(kq, so_ref[s, h])
            ks_parts.append(prod[:seq_len])
            qs_parts.append(prod[seq_len:])
        k_s = ks_parts[0] if nseq == 1 else jnp.concatenate(ks_parts, axis=0)
        q_s = qs_parts[0] if nseq == 1 else jnp.concatenate(qs_parts, axis=0)

        v_new = _dot(tinv, beta * (v - e_g * k_s))
        o = e_g * q_s + _dot(attn, v_new)

        k_dec = k * jnp.exp(g_l - g_c)
        for s in range(nseq):
            rows = slice(s * seq_len, (s + 1) * seq_len)
            carry = jnp.exp(g_l[s * seq_len:s * seq_len + 1, :])
            so_ref[s, h] = so_ref[s, h] * carry + _dot_tn(k_dec[rows], v_new[rows])

        o = o * lax.rsqrt(jnp.mean(o * o, axis=-1, keepdims=True) + EPS) * onorm_ref[...]
        zl = slice(h * HEAD_V, (h + 1) * HEAD_V)
        d_ref[:, zl] = o * _silu(z_ref[:, zl])


def _mixer(u, qkv, z, gbc, gbr, convp, poolp, s0, w, *, nseq, seq_len, n_valid, pos0,
           n_groups, n_steps, row_block0, shared_state):
    assert nseq * seq_len == SLAB and (nseq == 1 or n_steps == 1)
    tok = lambda gi, t: (row_block0 + gi * n_steps + t, 0)
    tok_t = lambda gi, t: (0, row_block0 + gi * n_steps + t)
    out_tok = lambda gi, t: (gi * n_steps + t, 0)
    st3 = (lambda gi, t: (0, 0, 0)) if shared_state else (lambda gi, t: (gi, 0, 0))
    st4 = (lambda gi, t: (0, 0, 0, 0)) if shared_state else (lambda gi, t: (gi, 0, 0, 0))
    c2 = lambda gi, t: (0, 0)
    c3 = lambda gi, t: (0, 0, 0)
    in_specs = [
        pl.BlockSpec((SLAB, POOL_WIDTH), tok),
        pl.BlockSpec((SLAB, QKV_W), tok),
        pl.BlockSpec((SLAB, VAL_W), tok),
        pl.BlockSpec((SLAB, GB_W), tok),
        pl.BlockSpec((2 * N_HEADS, SLAB), tok_t),
        pl.BlockSpec((nseq, CONV_W - 1, QKV_W), st3),
        pl.BlockSpec((nseq, POOL_STATE, POOL_WIDTH), st3),
        pl.BlockSpec((nseq, N_HEADS, HEAD_K, HEAD_V), st4),
        pl.BlockSpec((CONV_W, QKV_W), c2),
        pl.BlockSpec((1, HEAD_V), c2),
        pl.BlockSpec((POOL_GROUPS, POOL_GROUP_W, POOL_GROUP_W), c3),
        pl.BlockSpec((1, POOL_WIDTH), c2),
    ]
    n_seq_total = n_groups * nseq
    n_rows = n_groups * n_steps * SLAB
    out_specs = [
        pl.BlockSpec((SLAB, POOL_WIDTH), out_tok),
        pl.BlockSpec((SLAB, VAL_W), out_tok),
        pl.BlockSpec((nseq, CONV_W - 1, QKV_W), lambda gi, t: (gi, 0, 0)),
        pl.BlockSpec((nseq, POOL_STATE, POOL_WIDTH), lambda gi, t: (gi, 0, 0)),
        pl.BlockSpec((nseq, N_HEADS, HEAD_K, HEAD_V), lambda gi, t: (gi, 0, 0, 0)),
    ]
    out_shape = [
        jax.ShapeDtypeStruct((n_rows, POOL_WIDTH), F32),
        jax.ShapeDtypeStruct((n_rows, VAL_W), F32),
        jax.ShapeDtypeStruct((n_seq_total, CONV_W - 1, QKV_W), F32),
        jax.ShapeDtypeStruct((n_seq_total, POOL_STATE, POOL_WIDTH), F32),
        jax.ShapeDtypeStruct((n_seq_total, N_HEADS, HEAD_K, HEAD_V), F32),
    ]
    scratch = [
        pltpu.VMEM((nseq, SUBLANES + seq_len, QKV_W), F32),
        pltpu.VMEM((nseq, 2 * SUBLANES + seq_len, POOL_WIDTH), F32),
    ]
    return pl.pallas_call(
        functools.partial(_mixer_kernel, nseq, seq_len, n_valid, pos0, n_steps),
        grid=(n_groups, n_steps),
        in_specs=in_specs,
        out_specs=out_specs,
        out_shape=out_shape,
        scratch_shapes=scratch,
        compiler_params=pltpu.CompilerParams(
            dimension_semantics=("parallel", "arbitrary"), vmem_limit_bytes=VMEM_LIMIT),
        name="mixer_%dx%d" % (nseq, seq_len),
    )(u, qkv, z, gbc, gbr, convp, poolp, s0, w["conv_w"], w["o_norm_g"], w["pool_w"], w["pool_scale"])


def _out_ffn_kernel(x_ref, a_ref, d_ref, gates_ref, wbp_ref, wbd_ref, wout_ref, wup_ref, wdown_ref,
                    gpm_ref, gpf_ref, gqf_ref, y_ref):
    br_pool = jnp.dot(a_ref[...].astype(BF16), wbp_ref[...], preferred_element_type=F32)
    br_delta = jnp.dot(d_ref[...].astype(BF16), wbd_ref[...], preferred_element_type=F32)
    gate_pool = _sigmoid(gates_ref[:, :D_MODEL])
    gate_delta = _sigmoid(gates_ref[:, D_MODEL:])
    merged = (gate_pool * br_pool + gate_delta * br_delta).astype(BF16)
    m = jnp.dot(merged, wout_ref[...], preferred_element_type=F32)
    x = x_ref[...] + _rms(m, gpm_ref[...])
    h2 = _rms(x, gpf_ref[...]).astype(BF16)
    up = jnp.dot(h2, wup_ref[...], preferred_element_type=F32)
    act = jnp.square(jnp.maximum(up, 0.0)).astype(BF16)
    f = jnp.dot(act, wdown_ref[...], preferred_element_type=F32)
    y_ref[...] = x + _rms(f, gqf_ref[...])


def _out_ffn(x, a, d, gates, w, tm):
    n = x.shape[0]
    assert n % tm == 0
    const = lambda i: (0, 0)
    tok = lambda i: (i, 0)
    resident = pl.Buffered(1)
    in_specs = [
        pl.BlockSpec((tm, D_MODEL), tok),
        pl.BlockSpec((tm, POOL_WIDTH), tok),
        pl.BlockSpec((tm, VAL_W), tok),
        pl.BlockSpec((tm, 2 * D_MODEL), tok),
        pl.BlockSpec((POOL_WIDTH, D_MODEL), const, pipeline_mode=resident),
        pl.BlockSpec((VAL_W, D_MODEL), const, pipeline_mode=resident),
        pl.BlockSpec((D_MODEL, D_MODEL), const, pipeline_mode=resident),
        pl.BlockSpec((D_MODEL, D_FF), const, pipeline_mode=resident),
        pl.BlockSpec((D_FF, D_MODEL), const, pipeline_mode=resident),
        pl.BlockSpec((1, D_MODEL), const),
        pl.BlockSpec((1, D_MODEL), const),
        pl.BlockSpec((1, D_MODEL), const),
    ]
    return pl.pallas_call(
        _out_ffn_kernel,
        grid=(n // tm,),
        in_specs=in_specs,
        out_specs=pl.BlockSpec((tm, D_MODEL), tok),
        out_shape=jax.ShapeDtypeStruct((n, D_MODEL), F32),
        compiler_params=pltpu.CompilerParams(
            dimension_semantics=("parallel",), vmem_limit_bytes=VMEM_LIMIT),
        name="out_ffn",
    )(x, a, d, gates, w["wbp"], w["wbd"], w["wout"], w["wup"], w["wdown"],
      w["g_post_mix"], w["g_pre_ffn"], w["g_post_ffn"])


def _layer_weights(l, g_pre_mix, w_in, conv_w, a_log, dt_bias, o_norm_g, pool_w, pool_scale,
                   w_branch_pool, w_branch_delta, w_out, g_post_mix, g_pre_ffn, w_up, w_down,
                   g_post_ffn):
    wi = w_in[l]
    o_qkv = POOL_WIDTH
    o_b = o_qkv + QKV_W
    o_z = o_b + 2 * N_HEADS
    o_g = o_z + VAL_W
    wba = wi[:, o_b:o_z]
    zeros8 = jnp.zeros((N_HEADS,), F32)
    a_pad = jnp.concatenate([zeros8, a_log[l]])
    dt_pad = jnp.concatenate([zeros8, dt_bias[l]])
    pcol = jnp.zeros((SUBLANES, GB_W), F32).at[0, :2 * N_HEADS].set(a_pad).at[1, :2 * N_HEADS].set(dt_pad)
    prow = jnp.zeros((2 * N_HEADS, LANES), F32).at[:, 0].set(a_pad).at[:, 1].set(dt_pad)
    return {
        "g_pre_mix": g_pre_mix[l][None],
        "wu": wi[:, :o_qkv].astype(BF16),
        "wqkv": wi[:, o_qkv:o_b].astype(BF16),
        "wz": wi[:, o_z:o_g].astype(BF16),
        "wg": wi[:, o_g:].astype(BF16),
        "wba": jnp.pad(wba, ((0, 0), (0, GB_W - 2 * N_HEADS))).astype(BF16),
        "wbat": wba.T.astype(BF16),
        "pcol": pcol,
        "prow": prow,
        "conv_w": conv_w[l],
        "o_norm_g": o_norm_g[l][None],
        "pool_w": pool_w[l].astype(BF16),
        "pool_scale": pool_scale[l][None],
        "wbp": w_branch_pool[l].astype(BF16),
        "wbd": w_branch_delta[l].astype(BF16),
        "wout": w_out[l].astype(BF16),
        "wup": w_up[l].astype(BF16),
        "wdown": w_down[l].astype(BF16),
        "g_post_mix": g_post_mix[l][None],
        "g_pre_ffn": g_pre_ffn[l][None],
        "g_post_ffn": g_post_ffn[l][None],
    }


def kernel(x_prompt, x_sample, state_conv, state_ssm, state_pool, meta_tokens, g_pre_mix, w_in, conv_w, a_log, dt_bias, o_norm_g, pool_w, pool_scale, w_branch_pool, w_branch_delta, w_out, g_post_mix, g_pre_ffn, w_up, w_down, g_post_ffn):
    bp, seq, _ = x_prompt.shape
    bs, dec_seq, _ = x_sample.shape
    depth = w_in.shape[0]
    n_prompt = bp * seq
    n_sample = bs * dec_seq
    assert seq % SLAB == 0 and n_sample % SLAB == 0 and SLAB % dec_seq == 0 and N_META <= SLAB
    xp = x_prompt.reshape(n_prompt, D_MODEL)
    xs = jnp.concatenate([x_sample.reshape(n_sample, D_MODEL), meta_tokens.astype(F32),
                          jnp.zeros((SLAB - N_META, D_MODEL), F32)], axis=0)
    meta_block = n_sample // SLAB
    zero_conv = jnp.zeros((1, CONV_W - 1, QKV_W), F32)
    zero_pool = jnp.zeros((1, POOL_STATE, POOL_WIDTH), F32)
    zero_ssm = jnp.zeros((1, N_HEADS, HEAD_K, HEAD_V), F32)
    outs = [[] for _ in range(6)]
    for l in range(depth):
        w = _layer_weights(l, g_pre_mix, w_in, conv_w, a_log, dt_bias, o_norm_g, pool_w, pool_scale,
                           w_branch_pool, w_branch_delta, w_out, g_post_mix, g_pre_ffn, w_up,
                           w_down, g_post_ffn)
        up, qkvp, zp, gatesp, gbcp, gbrp = _in_proj(xp, w["g_pre_mix"], w, 256)
        us, qkvs, zs, gatess, gbcs, gbrs = _in_proj(xs, w["g_pre_mix"], w, SLAB)
        a_m, d_m, conv_m, pool_m, ssm_m = _mixer(
            us, qkvs, zs, gbcs, gbrs, zero_conv, zero_pool, zero_ssm, w,
            nseq=1, seq_len=SLAB, n_valid=N_META, pos0=0, n_groups=1, n_steps=1,
            row_block0=meta_block, shared_state=True)
        a_s, d_s, conv_s, pool_s, ssm_s = _mixer(
            us, qkvs, zs, gbcs, gbrs, state_conv[l], state_pool[l], state_ssm[l].astype(F32), w,
            nseq=SLAB // dec_seq, seq_len=dec_seq, n_valid=dec_seq, pos0=PAST_LEN,
            n_groups=n_sample // SLAB, n_steps=1, row_block0=0, shared_state=False)
        a_p, d_p, conv_p, pool_p, ssm_p = _mixer(
            up, qkvp, zp, gbcp, gbrp, conv_m, pool_m, ssm_m, w,
            nseq=1, seq_len=SLAB, n_valid=SLAB, pos0=N_META, n_groups=bp, n_steps=seq // SLAB,
            row_block0=0, shared_state=True)
        xp = _out_ffn(xp, a_p, d_p, gatesp, w, 256)
        xs = _out_ffn(xs, jnp.concatenate([a_s, a_m], axis=0), jnp.concatenate([d_s, d_m], axis=0),
                      gatess, w, SLAB)
        for acc, val in zip(outs, (conv_p, ssm_p, pool_p, conv_s, ssm_s, pool_s)):
            acc.append(val)
    y_prompt = xp.reshape(bp, seq, D_MODEL)
    y_sample = xs[:n_sample].reshape(bs, dec_seq, D_MODEL)
    return (y_prompt, y_sample) + tuple(jnp.stack(o) for o in outs)
```

```python
import functools

import jax
import jax.numpy as jnp
from jax import lax
from jax.experimental import pallas as pl
from jax.experimental.pallas import tpu as pltpu

D_MODEL = 1024
N_META = 16
POOL_GROUPS = 4
POOL_GROUP_W = 128
POOL_WIDTH = 512
POOL_WINDOWS = (2, 4, 8, 16)
POOL_STATE = 15
HEAD_K = 128
HEAD_V = 128
N_HEADS = 8
KEY_W = 1024
VAL_W = 1024
QKV_W = 3072
CONV_W = 4
D_FF = 4096
PAST_LEN = 16384
EPS = 1e-6

LANES = 128
SUBLANES = 8
SLAB = 128
GB_W = 128
VMEM_LIMIT = 56 * 1024 * 1024

F32 = jnp.float32
BF16 = jnp.bfloat16
HI = lax.Precision.HIGHEST
NEG_BIG = -1e30


def _dot(a, b):
    return jnp.dot(a.astype(BF16), b.astype(BF16), preferred_element_type=F32)


def _dot_nt(a, b):
    return lax.dot_general(a.astype(BF16), b.astype(BF16), (((1,), (1,)), ((), ())),
                           preferred_element_type=F32)


def _dot_tn(a, b):
    return lax.dot_general(a.astype(BF16), b.astype(BF16), (((0,), (0,)), ((), ())),
                           preferred_element_type=F32)


def _dot_hi(a, b):
    return jnp.dot(a, b, precision=HI, preferred_element_type=F32)


def _rms(x, g):
    return x * lax.rsqrt(jnp.mean(x * x, axis=-1, keepdims=True) + EPS) * g


def _sigmoid(x):
    return 1.0 / (1.0 + jnp.exp(-x))


def _silu(x):
    return x * _sigmoid(x)


def _softplus(x):
    return jnp.maximum(x, 0.0) + jnp.log1p(jnp.exp(-jnp.abs(x)))


def _in_proj_kernel(x_ref, g_ref, wu_ref, wqkv_ref, wz_ref, wg_ref, wba_ref, wbat_ref,
                    pcol_ref, prow_ref,
                    u_ref, qkv_ref, z_ref, gates_ref, gbc_ref, gbr_ref):
    h = _rms(x_ref[...], g_ref[...]).astype(BF16)
    u_ref[...] = jnp.dot(h, wu_ref[...], preferred_element_type=F32)
    qkv_ref[...] = jnp.dot(h, wqkv_ref[...], preferred_element_type=F32)
    z_ref[...] = jnp.dot(h, wz_ref[...], preferred_element_type=F32)
    gates_ref[...] = jnp.dot(h, wg_ref[...], preferred_element_type=F32)
    ba = jnp.dot(h, wba_ref[...], preferred_element_type=F32)
    lane = lax.broadcasted_iota(jnp.int32, ba.shape, 1)
    neg_a = -jnp.exp(pcol_ref[0:1, :])
    g_col = neg_a * _softplus(ba + pcol_ref[1:2, :])
    gbc_ref[...] = jnp.where(lane < N_HEADS, _sigmoid(ba), jnp.where(lane < 2 * N_HEADS, g_col, 0.0))
    bat = lax.dot_general(wbat_ref[...], h, (((1,), (1,)), ((), ())),
                          preferred_element_type=F32)
    row = lax.broadcasted_iota(jnp.int32, bat.shape, 0)
    neg_a_r = -jnp.exp(prow_ref[:, 0:1])
    g_row = neg_a_r * _softplus(bat + prow_ref[:, 1:2])
    gbr_ref[...] = jnp.where(row < N_HEADS, _sigmoid(bat), g_row)


def _in_proj(x, g, w, tm):
    n = x.shape[0]
    assert n % tm == 0
    const = lambda i: (0, 0)
    tok = lambda i: (i, 0)
    in_specs = [
        pl.BlockSpec((tm, D_MODEL), tok),
        pl.BlockSpec((1, D_MODEL), const),
        pl.BlockSpec((D_MODEL, POOL_WIDTH), const),
        pl.BlockSpec((D_MODEL, QKV_W), const),
        pl.BlockSpec((D_MODEL, VAL_W), const),
        pl.BlockSpec((D_MODEL, 2 * D_MODEL), const),
        pl.BlockSpec((D_MODEL, GB_W), const),
        pl.BlockSpec((2 * N_HEADS, D_MODEL), const),
        pl.BlockSpec((SUBLANES, GB_W), const),
        pl.BlockSpec((2 * N_HEADS, LANES), const),
    ]
    out_specs = [
        pl.BlockSpec((tm, POOL_WIDTH), tok),
        pl.BlockSpec((tm, QKV_W), tok),
        pl.BlockSpec((tm, VAL_W), tok),
        pl.BlockSpec((tm, 2 * D_MODEL), tok),
        pl.BlockSpec((tm, GB_W), tok),
        pl.BlockSpec((2 * N_HEADS, tm), lambda i: (0, i)),
    ]
    out_shape = [
        jax.ShapeDtypeStruct((n, POOL_WIDTH), F32),
        jax.ShapeDtypeStruct((n, QKV_W), F32),
        jax.ShapeDtypeStruct((n, VAL_W), F32),
        jax.ShapeDtypeStruct((n, 2 * D_MODEL), F32),
        jax.ShapeDtypeStruct((n, GB_W), F32),
        jax.ShapeDtypeStruct((2 * N_HEADS, n), F32),
    ]
    return pl.pallas_call(
        _in_proj_kernel,
        grid=(n // tm,),
        in_specs=in_specs,
        out_specs=out_specs,
        out_shape=out_shape,
        compiler_params=pltpu.CompilerParams(
            dimension_semantics=("parallel",), vmem_limit_bytes=VMEM_LIMIT),
        name="in_proj",
    )(x, g, w["wu"], w["wqkv"], w["wz"], w["wg"], w["wba"], w["wbat"], w["pcol"], w["prow"])


def _mixer_kernel(nseq, seq_len, n_valid, pos0, n_steps,
                  u_ref, qkv_ref, z_ref, gbc_ref, gbr_ref, convp_ref, poolp_ref, s0_ref,
                  convw_ref, onorm_ref, poolw_ref, pscale_ref,
                  a_ref, d_ref, convo_ref, poolo_ref, so_ref,
                  qkv_ext, u_ext):
    t = pl.program_id(1)
    conv_base = SUBLANES
    pool_base = 2 * SUBLANES

    if n_steps > 1:
        @pl.when(t > 0)
        def _():
            qkv_ext[0, conv_base - (CONV_W - 1):conv_base, :] = (
                qkv_ext[0, conv_base + seq_len - (CONV_W - 1):conv_base + seq_len, :])
            u_ext[0, pool_base - POOL_STATE:pool_base, :] = (
                u_ext[0, pool_base + seq_len - POOL_STATE:pool_base + seq_len, :])

    @pl.when(t == 0)
    def _():
        for s in range(nseq):
            qkv_ext[s, conv_base - (CONV_W - 1):conv_base, :] = convp_ref[s]
            u_ext[s, pool_base - POOL_STATE:pool_base, :] = poolp_ref[s]
        so_ref[...] = s0_ref[...]

    for s in range(nseq):
        qkv_ext[s, conv_base:conv_base + seq_len, :] = qkv_ref[s * seq_len:(s + 1) * seq_len, :]
        u_ext[s, pool_base:pool_base + seq_len, :] = u_ref[s * seq_len:(s + 1) * seq_len, :]

    last = n_valid if n_valid < seq_len else seq_len
    for s in range(nseq):
        convo_ref[s] = qkv_ext[s, conv_base + last - (CONV_W - 1):conv_base + last, :]
        poolo_ref[s] = u_ext[s, pool_base + last - POOL_STATE:pool_base + last, :]

    for gi, win in enumerate(POOL_WINDOWS):
        lanes = slice(gi * POOL_GROUP_W, (gi + 1) * POOL_GROUP_W)
        if pos0 >= POOL_STATE:
            cnt = float(win)
        else:
            pos = lax.broadcasted_iota(jnp.int32, (seq_len, 1), 0) + (pos0 + 1) + t * seq_len
            cnt = jnp.minimum(pos, win).astype(F32)
        pieces = []
        for s in range(nseq):
            tok = u_ext[s, pool_base:pool_base + seq_len, lanes]
            tot = tok
            for j in range(1, win):
                tot = tot + u_ext[s, pool_base - j:pool_base - j + seq_len, lanes]
            pieces.append(tot / cnt - tok)
        dev = pieces[0] if nseq == 1 else jnp.concatenate(pieces, axis=0)
        a_ref[:, lanes] = _dot(dev, poolw_ref[gi]) * pscale_ref[:, lanes]

    gbc = gbc_ref[...]
    gbr = gbr_ref[...]
    if n_valid < seq_len:
        gbc = jnp.where(lax.broadcasted_iota(jnp.int32, gbc.shape, 0) < n_valid, gbc, 0.0)
        gbr = jnp.where(lax.broadcasted_iota(jnp.int32, gbr.shape, 1) < n_valid, gbr, 0.0)
    ri = lax.broadcasted_iota(jnp.int32, (SLAB, SLAB), 0)
    ci = lax.broadcasted_iota(jnp.int32, (SLAB, SLAB), 1)
    shift = seq_len.bit_length() - 1
    same = (ri >> shift) == (ci >> shift)
    m_le = same & (ci <= ri)
    m_lt = same & (ci < ri)
    gc_col = _dot_hi(m_le.astype(F32), gbc)
    gc_row = _dot_hi(gbr, (same & (ri <= ci)).astype(F32))
    gl_col = _dot_hi(same.astype(F32), gbc)

    def conv(lanes):
        pieces = []
        for s in range(nseq):
            acc = None
            for j in range(CONV_W):
                lo = conv_base - (CONV_W - 1) + j
                term = qkv_ext[s, lo:lo + seq_len, lanes] * convw_ref[j:j + 1, lanes]
                acc = term if acc is None else acc + term
            pieces.append(acc)
        c = pieces[0] if nseq == 1 else jnp.concatenate(pieces, axis=0)
        return _silu(c)

    heads = range(N_HEADS)
    eye = (ri == ci).astype(F32)
    narrow = (lambda a: a.astype(BF16)) if seq_len % (2 * SUBLANES) == 0 else (lambda a: a)
    q_bf, k_bf, kb_bf, v, beta, e_g, k_dec, carry = [], [], [], [], [], [], [], []
    decay_le, decay_lt = [], []
    for h in heads:
        q = conv(slice(h * HEAD_K, (h + 1) * HEAD_K))
        k = conv(slice(KEY_W + h * HEAD_K, KEY_W + (h + 1) * HEAD_K))
        v.append(conv(slice(2 * KEY_W + h * HEAD_V, 2 * KEY_W + (h + 1) * HEAD_V)))
        q = q * (lax.rsqrt(jnp.sum(q * q, axis=-1, keepdims=True) + EPS) * (HEAD_K ** -0.5))
        k = k * lax.rsqrt(jnp.sum(k * k, axis=-1, keepdims=True) + EPS)
        b = gbc[:, h:h + 1]
        g_c = gc_col[:, N_HEADS + h:N_HEADS + h + 1]
        g_r = gc_row[N_HEADS + h:N_HEADS + h + 1, :]
        g_l = gl_col[:, N_HEADS + h:N_HEADS + h + 1]
        d_le = jnp.exp(jnp.where(m_le, g_c - g_r, NEG_BIG))
        decay_le.append(d_le)
        decay_lt.append(jnp.where(m_lt, d_le, 0.0))
        beta.append(b)
        e_g.append(jnp.exp(g_c))
        q_bf.append(narrow(q))
        k_bf.append(narrow(k))
        kb_bf.append((k * b).astype(BF16))
        k_dec.append(narrow(k * jnp.exp(g_l - g_c)))
        carry.append([jnp.exp(g_l[s * seq_len:s * seq_len + 1, :]) for s in range(nseq)])

    lmat = [_dot_nt(kb_bf[h], k_bf[h]) * decay_lt[h] for h in heads]
    attn = [(_dot_nt(q_bf[h], k_bf[h]) * decay_le[h]).astype(BF16) for h in heads]

    k_s, q_s = [], []
    for h in heads:
        ks_parts, qs_parts = [], []
        for s in range(nseq):
            rows = slice(s * seq_len, (s + 1) * seq_len)
            kq = jnp.concatenate([k_bf[h][rows], q_bf[h][rows]], axis=0)
            prod = _dot(kq, so_ref[s, h])
            ks_parts.append(prod[:seq_len])
            qs_parts.append(prod[seq_len:])
        k_s.append(ks_parts[0] if nseq == 1 else jnp.concatenate(ks_parts, axis=0))
        q_s.append(qs_parts[0] if nseq == 1 else jnp.concatenate(qs_parts, axis=0))
    rhs = [(beta[h] * (v[h] - e_g[h] * k_s[h])).astype(BF16) for h in heads]

    pair = (ri >> 1) == (ci >> 1)
    l_bf = [lmat[h].astype(BF16) for h in heads]
    inv = [eye - jnp.where(pair, lmat[h], 0.0) for h in heads]
    blk = 2
    while blk < seq_len:
        sh = blk.bit_length() - 1
        lower_left = ((ri >> (sh + 1)) == (ci >> (sh + 1))) & ((ri >> sh) != (ci >> sh))
        inv_bf = [inv[h].astype(BF16) for h in heads]
        half = [_dot(inv_bf[h], l_bf[h]).astype(BF16) for h in heads]
        inv = [inv[h] - jnp.where(lower_left, _dot(half[h], inv_bf[h]), 0.0) for h in heads]
        blk *= 2

    v_new = [narrow(_dot(inv[h], rhs[h])) for h in heads]
    for h in heads:
        for s in range(nseq):
            rows = slice(s * seq_len, (s + 1) * seq_len)
            so_ref[s, h] = so_ref[s, h] * carry[h][s] + _dot_tn(k_dec[h][rows], v_new[h][rows])
    for h in heads:
        o = e_g[h] * q_s[h] + _dot(attn[h], v_new[h])
        o = o * lax.rsqrt(jnp.mean(o * o, axis=-1, keepdims=True) + EPS) * onorm_ref[...]
        zl = slice(h * HEAD_V, (h + 1) * HEAD_V)
        d_ref[:, zl] = o * _silu(z_ref[:, zl])


def _mixer(u, qkv, z, gbc, gbr, convp, poolp, s0, w, *, nseq, seq_len, n_valid, pos0,
           n_groups, n_steps, row_block0, shared_state):
    assert nseq * seq_len == SLAB and (nseq == 1 or n_steps == 1)
    tok = lambda gi, t: (row_block0 + gi * n_steps + t, 0)
    tok_t = lambda gi, t: (0, row_block0 + gi * n_steps + t)
    out_tok = lambda gi, t: (gi * n_steps + t, 0)
    st3 = (lambda gi, t: (0, 0, 0)) if shared_state else (lambda gi, t: (gi, 0, 0))
    st4 = (lambda gi, t: (0, 0, 0, 0)) if shared_state else (lambda gi, t: (gi, 0, 0, 0))
    c2 = lambda gi, t: (0, 0)
    c3 = lambda gi, t: (0, 0, 0)
    in_specs = [
        pl.BlockSpec((SLAB, POOL_WIDTH), tok),
        pl.BlockSpec((SLAB, QKV_W), tok),
        pl.BlockSpec((SLAB, VAL_W), tok),
        pl.BlockSpec((SLAB, GB_W), tok),
        pl.BlockSpec((2 * N_HEADS, SLAB), tok_t),
        pl.BlockSpec((nseq, CONV_W - 1, QKV_W), st3),
        pl.BlockSpec((nseq, POOL_STATE, POOL_WIDTH), st3),
        pl.BlockSpec((nseq, N_HEADS, HEAD_K, HEAD_V), st4),
        pl.BlockSpec((CONV_W, QKV_W), c2),
        pl.BlockSpec((1, HEAD_V), c2),
        pl.BlockSpec((POOL_GROUPS, POOL_GROUP_W, POOL_GROUP_W), c3),
        pl.BlockSpec((1, POOL_WIDTH), c2),
    ]
    n_seq_total = n_groups * nseq
    n_rows = n_groups * n_steps * SLAB
    out_specs = [
        pl.BlockSpec((SLAB, POOL_WIDTH), out_tok),
        pl.BlockSpec((SLAB, VAL_W), out_tok),
        pl.BlockSpec((nseq, CONV_W - 1, QKV_W), lambda gi, t: (gi, 0, 0)),
        pl.BlockSpec((nseq, POOL_STATE, POOL_WIDTH), lambda gi, t: (gi, 0, 0)),
        pl.BlockSpec((nseq, N_HEADS, HEAD_K, HEAD_V), lambda gi, t: (gi, 0, 0, 0)),
    ]
    out_shape = [
        jax.ShapeDtypeStruct((n_rows, POOL_WIDTH), F32),
        jax.ShapeDtypeStruct((n_rows, VAL_W), F32),
        jax.ShapeDtypeStruct((n_seq_total, CONV_W - 1, QKV_W), F32),
        jax.ShapeDtypeStruct((n_seq_total, POOL_STATE, POOL_WIDTH), F32),
        jax.ShapeDtypeStruct((n_seq_total, N_HEADS, HEAD_K, HEAD_V), F32),
    ]
    scratch = [
        pltpu.VMEM((nseq, SUBLANES + seq_len, QKV_W), F32),
        pltpu.VMEM((nseq, 2 * SUBLANES + seq_len, POOL_WIDTH), F32),
    ]
    return pl.pallas_call(
        functools.partial(_mixer_kernel, nseq, seq_len, n_valid, pos0, n_steps),
        grid=(n_groups, n_steps),
        in_specs=in_specs,
        out_specs=out_specs,
        out_shape=out_shape,
        scratch_shapes=scratch,
        compiler_params=pltpu.CompilerParams(
            dimension_semantics=("parallel", "arbitrary"), vmem_limit_bytes=VMEM_LIMIT),
        name="mixer_%dx%d" % (nseq, seq_len),
    )(u, qkv, z, gbc, gbr, convp, poolp, s0, w["conv_w"], w["o_norm_g"], w["pool_w"], w["pool_scale"])


def _out_ffn_kernel(x_ref, a_ref, d_ref, gates_ref, wbp_ref, wbd_ref, wout_ref, wup_ref, wdown_ref,
                    gpm_ref, gpf_ref, gqf_ref, y_ref):
    br_pool = jnp.dot(a_ref[...].astype(BF16), wbp_ref[...], preferred_element_type=F32)
    br_delta = jnp.dot(d_ref[...].astype(BF16), wbd_ref[...], preferred_element_type=F32)
    gate_pool = _sigmoid(gates_ref[:, :D_MODEL])
    gate_delta = _sigmoid(gates_ref[:, D_MODEL:])
    merged = (gate_pool * br_pool + gate_delta * br_delta).astype(BF16)
    m = jnp.dot(merged, wout_ref[...], preferred_element_type=F32)
    x = x_ref[...] + _rms(m, gpm_ref[...])
    h2 = _rms(x, gpf_ref[...]).astype(BF16)
    up = jnp.dot(h2, wup_ref[...], preferred_element_type=F32)
    act = jnp.square(jnp.maximum(up, 0.0)).astype(BF16)
    f = jnp.dot(act, wdown_ref[...], preferred_element_type=F32)
    y_ref[...] = x + _rms(f, gqf_ref[...])


def _out_ffn(x, a, d, gates, w, tm):
    n = x.shape[0]
    assert n % tm == 0
    const = lambda i: (0, 0)
    tok = lambda i: (i, 0)
    resident = pl.Buffered(1)
    in_specs = [
        pl.BlockSpec((tm, D_MODEL), tok),
        pl.BlockSpec((tm, POOL_WIDTH), tok),
        pl.BlockSpec((tm, VAL_W), tok),
        pl.BlockSpec((tm, 2 * D_MODEL), tok),
        pl.BlockSpec((POOL_WIDTH, D_MODEL), const, pipeline_mode=resident),
        pl.BlockSpec((VAL_W, D_MODEL), const, pipeline_mode=resident),
        pl.BlockSpec((D_MODEL, D_MODEL), const, pipeline_mode=resident),
        pl.BlockSpec((D_MODEL, D_FF), const, pipeline_mode=resident),
        pl.BlockSpec((D_FF, D_MODEL), const, pipeline_mode=resident),
        pl.BlockSpec((1, D_MODEL), const),
        pl.BlockSpec((1, D_MODEL), const),
        pl.BlockSpec((1, D_MODEL), const),
    ]
    return pl.pallas_call(
        _out_ffn_kernel,
        grid=(n // tm,),
        in_specs=in_specs,
        out_specs=pl.BlockSpec((tm, D_MODEL), tok),
        out_shape=jax.ShapeDtypeStruct((n, D_MODEL), F32),
        compiler_params=pltpu.CompilerParams(
            dimension_semantics=("parallel",), vmem_limit_bytes=VMEM_LIMIT),
        name="out_ffn",
    )(x, a, d, gates, w["wbp"], w["wbd"], w["wout"], w["wup"], w["wdown"],
      w["g_post_mix"], w["g_pre_ffn"], w["g_post_ffn"])


def _layer_weights(l, g_pre_mix, w_in, conv_w, a_log, dt_bias, o_norm_g, pool_w, pool_scale,
                   w_branch_pool, w_branch_delta, w_out, g_post_mix, g_pre_ffn, w_up, w_down,
                   g_post_ffn):
    wi = w_in[l]
    o_qkv = POOL_WIDTH
    o_b = o_qkv + QKV_W
    o_z = o_b + 2 * N_HEADS
    o_g = o_z + VAL_W
    wba = wi[:, o_b:o_z]
    zeros8 = jnp.zeros((N_HEADS,), F32)
    a_pad = jnp.concatenate([zeros8, a_log[l]])
    dt_pad = jnp.concatenate([zeros8, dt_bias[l]])
    pcol = jnp.zeros((SUBLANES, GB_W), F32).at[0, :2 * N_HEADS].set(a_pad).at[1, :2 * N_HEADS].set(dt_pad)
    prow = jnp.zeros((2 * N_HEADS, LANES), F32).at[:, 0].set(a_pad).at[:, 1].set(dt_pad)
    return {
        "g_pre_mix": g_pre_mix[l][None],
        "wu": wi[:, :o_qkv].astype(BF16),
        "wqkv": wi[:, o_qkv:o_b].astype(BF16),
        "wz": wi[:, o_z:o_g].astype(BF16),
        "wg": wi[:, o_g:].astype(BF16),
        "wba": jnp.pad(wba, ((0, 0), (0, GB_W - 2 * N_HEADS))).astype(BF16),
        "wbat": wba.T.astype(BF16),
        "pcol": pcol,
        "prow": prow,
        "conv_w": conv_w[l],
        "o_norm_g": o_norm_g[l][None],
        "pool_w": pool_w[l].astype(BF16),
        "pool_scale": pool_scale[l][None],
        "wbp": w_branch_pool[l].astype(BF16),
        "wbd": w_branch_delta[l].astype(BF16),
        "wout": w_out[l].astype(BF16),
        "wup": w_up[l].astype(BF16),
        "wdown": w_down[l].astype(BF16),
        "g_post_mix": g_post_mix[l][None],
        "g_pre_ffn": g_pre_ffn[l][None],
        "g_post_ffn": g_post_ffn[l][None],
    }


def kernel(x_prompt, x_sample, state_conv, state_ssm, state_pool, meta_tokens, g_pre_mix, w_in, conv_w, a_log, dt_bias, o_norm_g, pool_w, pool_scale, w_branch_pool, w_branch_delta, w_out, g_post_mix, g_pre_ffn, w_up, w_down, g_post_ffn):
    bp, seq, _ = x_prompt.shape
    bs, dec_seq, _ = x_sample.shape
    depth = w_in.shape[0]
    n_prompt = bp * seq
    n_sample = bs * dec_seq
    assert seq % SLAB == 0 and n_sample % SLAB == 0 and SLAB % dec_seq == 0 and N_META <= SLAB
    xp = x_prompt.reshape(n_prompt, D_MODEL)
    xs = jnp.concatenate([x_sample.reshape(n_sample, D_MODEL), meta_tokens.astype(F32),
                          jnp.zeros((SLAB - N_META, D_MODEL), F32)], axis=0)
    meta_block = n_sample // SLAB
    zero_conv = jnp.zeros((1, CONV_W - 1, QKV_W), F32)
    zero_pool = jnp.zeros((1, POOL_STATE, POOL_WIDTH), F32)
    zero_ssm = jnp.zeros((1, N_HEADS, HEAD_K, HEAD_V), F32)
    outs = [[] for _ in range(6)]
    for l in range(depth):
        w = _layer_weights(l, g_pre_mix, w_in, conv_w, a_log, dt_bias, o_norm_g, pool_w, pool_scale,
                           w_branch_pool, w_branch_delta, w_out, g_post_mix, g_pre_ffn, w_up,
                           w_down, g_post_ffn)
        up, qkvp, zp, gatesp, gbcp, gbrp = _in_proj(xp, w["g_pre_mix"], w, 256)
        us, qkvs, zs, gatess, gbcs, gbrs = _in_proj(xs, w["g_pre_mix"], w, SLAB)
        a_m, d_m, conv_m, pool_m, ssm_m = _mixer(
            us, qkvs, zs, gbcs, gbrs, zero_conv, zero_pool, zero_ssm, w,
            nseq=1, seq_len=SLAB, n_valid=N_META, pos0=0, n_groups=1, n_steps=1,
            row_block0=meta_block, shared_state=True)
        a_s, d_s, conv_s, pool_s, ssm_s = _mixer(
            us, qkvs, zs, gbcs, gbrs, state_conv[l], state_pool[l], state_ssm[l].astype(F32), w,
            nseq=SLAB // dec_seq, seq_len=dec_seq, n_valid=dec_seq, pos0=PAST_LEN,
            n_groups=n_sample // SLAB, n_steps=1, row_block0=0, shared_state=False)
        a_p, d_p, conv_p, pool_p, ssm_p = _mixer(
            up, qkvp, zp, gbcp, gbrp, conv_m, pool_m, ssm_m, w,
            nseq=1, seq_len=SLAB, n_valid=SLAB, pos0=N_META, n_groups=bp, n_steps=seq // SLAB,
            row_block0=0, shared_state=True)
        xp = _out_ffn(xp, a_p, d_p, gatesp, w, 256)
        xs = _out_ffn(xs, jnp.concatenate([a_s, a_m], axis=0), jnp.concatenate([d_s, d_m], axis=0),
                      gatess, w, SLAB)
        for acc, val in zip(outs, (conv_p, ssm_p, pool_p, conv_s, ssm_s, pool_s)):
            acc.append(val)
    y_prompt = xp.reshape(bp, seq, D_MODEL)
    y_sample = xs[:n_sample].reshape(bs, dec_seq, D_MODEL)
    return (y_prompt, y_sample) + tuple(jnp.stack(o) for o in outs)
```

```python
import functools

import jax
import jax.numpy as jnp
from jax import lax
from jax.experimental import pallas as pl
from jax.experimental.pallas import tpu as pltpu

D_MODEL = 1024
N_META = 16
POOL_GROUPS = 4
POOL_GROUP_W = 128
POOL_WIDTH = 512
POOL_WINDOWS = (2, 4, 8, 16)
POOL_STATE = 15
HEAD_K = 128
HEAD_V = 128
N_HEADS = 8
KEY_W = 1024
VAL_W = 1024
QKV_W = 3072
CONV_W = 4
D_FF = 4096
PAST_LEN = 16384
EPS = 1e-6

LANES = 128
SUBLANES = 8
SLAB = 128
GB_W = 128
VMEM_LIMIT = 56 * 1024 * 1024

F32 = jnp.float32
BF16 = jnp.bfloat16
HI = lax.Precision.HIGHEST
NEG_BIG = -1e30
NEG_LOG2_E = -1.4426950408889634


def _dot(a, b):
    return jnp.dot(a.astype(BF16), b.astype(BF16), preferred_element_type=F32)


def _dot_nt(a, b):
    return lax.dot_general(a.astype(BF16), b.astype(BF16), (((1,), (1,)), ((), ())),
                           preferred_element_type=F32)


def _dot_tn(a, b):
    return lax.dot_general(a.astype(BF16), b.astype(BF16), (((0,), (0,)), ((), ())),
                           preferred_element_type=F32)


def _dot_hi(a, b):
    return jnp.dot(a, b, precision=HI, preferred_element_type=F32)


def _rms(x, g):
    return x * lax.rsqrt(jnp.mean(x * x, axis=-1, keepdims=True) + EPS) * g


def _sigmoid(x):
    return 1.0 / (1.0 + jnp.exp2(x * NEG_LOG2_E))


def _silu(x):
    return x * _sigmoid(x)


def _softplus(x):
    return jnp.maximum(x, 0.0) + jnp.log1p(jnp.exp(-jnp.abs(x)))


def _l2_normalize(x, scale=1.0):
    return x * (lax.rsqrt(jnp.sum(x * x, axis=-1, keepdims=True) + EPS) * scale)


def _causal_conv_silu(ext_ref, seq, row0, n_rows, lanes, convw_ref):
    acc = None
    for j in range(CONV_W):
        lo = row0 - (CONV_W - 1) + j
        rows = ext_ref[lo:lo + n_rows, lanes] if seq is None else ext_ref[seq, lo:lo + n_rows, lanes]
        term = rows * convw_ref[j:j + 1, lanes]
        acc = term if acc is None else acc + term
    return _silu(acc)


def _beta_decay_cols(ba, pcol_ref):
    lane = lax.broadcasted_iota(jnp.int32, ba.shape, 1)
    g_col = -jnp.exp(pcol_ref[0:1, :]) * _softplus(ba + pcol_ref[1:2, :])
    return jnp.where(lane < N_HEADS, _sigmoid(ba), jnp.where(lane < 2 * N_HEADS, g_col, 0.0))


def _beta_decay_rows(bat, prow_ref):
    row = lax.broadcasted_iota(jnp.int32, bat.shape, 0)
    g_row = -jnp.exp(prow_ref[:, 0:1]) * _softplus(bat + prow_ref[:, 1:2])
    return jnp.where(row < N_HEADS, _sigmoid(bat), g_row)


def _in_proj_kernel(x_ref, g_ref, wu_ref, wqkv_ref, wz_ref, wg_ref, wba_ref, wbat_ref,
                    pcol_ref, prow_ref,
                    u_ref, qkv_ref, z_ref, gates_ref, gbc_ref, gbr_ref):
    h = _rms(x_ref[...], g_ref[...]).astype(BF16)
    u_ref[...] = jnp.dot(h, wu_ref[...], preferred_element_type=F32)
    qkv_ref[...] = jnp.dot(h, wqkv_ref[...], preferred_element_type=F32)
    z_ref[...] = jnp.dot(h, wz_ref[...], preferred_element_type=F32)
    gates_ref[...] = jnp.dot(h, wg_ref[...], preferred_element_type=F32)
    gbc_ref[...] = _beta_decay_cols(jnp.dot(h, wba_ref[...], preferred_element_type=F32), pcol_ref)
    bat = lax.dot_general(wbat_ref[...], h, (((1,), (1,)), ((), ())), preferred_element_type=F32)
    gbr_ref[...] = _beta_decay_rows(bat, prow_ref)


def _in_proj_conv_kernel(tm, n_tiles,
                         x_ref, g_ref, wu_ref, wqkv_ref, wz_ref, wg_ref, wba_ref, wbat_ref,
                         pcol_ref, prow_ref, convw_ref, convp_ref,
                         u_ref, qn_ref, kn_ref, v_ref, sz_ref, gates_ref, gbc_ref, gbr_ref, convo_ref,
                         ext):
    t = pl.program_id(1)
    base = SUBLANES
    if n_tiles > 1:
        @pl.when(t > 0)
        def _():
            ext[base - (CONV_W - 1):base, :] = ext[base + tm - (CONV_W - 1):base + tm, :]

    @pl.when(t == 0)
    def _():
        ext[base - (CONV_W - 1):base, :] = convp_ref[0]

    h = _rms(x_ref[...], g_ref[...]).astype(BF16)
    u_ref[...] = jnp.dot(h, wu_ref[...], preferred_element_type=F32)
    ext[base:base + tm, :] = jnp.dot(h, wqkv_ref[...], preferred_element_type=F32)
    convo_ref[0] = ext[base + tm - (CONV_W - 1):base + tm, :]
    for c in range(QKV_W // LANES):
        lanes = slice(c * LANES, (c + 1) * LANES)
        y = _causal_conv_silu(ext, None, base, tm, lanes, convw_ref)
        if c < N_HEADS:
            qn_ref[:, lanes] = _l2_normalize(y, HEAD_K ** -0.5)
        elif c < 2 * N_HEADS:
            kn_ref[:, slice((c - N_HEADS) * LANES, (c - N_HEADS + 1) * LANES)] = _l2_normalize(y)
        else:
            v_ref[:, slice((c - 2 * N_HEADS) * LANES, (c - 2 * N_HEADS + 1) * LANES)] = y
    sz_ref[...] = _silu(jnp.dot(h, wz_ref[...], preferred_element_type=F32))
    gates_ref[...] = jnp.dot(h, wg_ref[...], preferred_element_type=F32)
    gbc_ref[...] = _beta_decay_cols(jnp.dot(h, wba_ref[...], preferred_element_type=F32), pcol_ref)
    bat = lax.dot_general(wbat_ref[...], h, (((1,), (1,)), ((), ())), preferred_element_type=F32)
    gbr_ref[...] = _beta_decay_rows(bat, prow_ref)


def _in_proj(x, w, tm, conv_prefix=None, n_seq=1):
    n = x.shape[0]
    fused = conv_prefix is not None
    assert n % (tm * n_seq) == 0
    n_tiles = n // (tm * n_seq)
    if fused:
        grid = (n_seq, n_tiles)
        const = lambda s, t: (0, 0)
        tok = lambda s, t: (s * n_tiles + t, 0)
        tok_t = lambda s, t: (0, s * n_tiles + t)
    else:
        grid = (n_tiles,)
        const = lambda i: (0, 0)
        tok = lambda i: (i, 0)
        tok_t = lambda i: (0, i)
    resident = pl.Buffered(1)
    in_specs = [
        pl.BlockSpec((tm, D_MODEL), tok),
        pl.BlockSpec((1, D_MODEL), const),
        pl.BlockSpec((D_MODEL, POOL_WIDTH), const, pipeline_mode=resident),
        pl.BlockSpec((D_MODEL, QKV_W), const, pipeline_mode=resident),
        pl.BlockSpec((D_MODEL, VAL_W), const, pipeline_mode=resident),
        pl.BlockSpec((D_MODEL, 2 * D_MODEL), const, pipeline_mode=resident),
        pl.BlockSpec((D_MODEL, GB_W), const, pipeline_mode=resident),
        pl.BlockSpec((2 * N_HEADS, D_MODEL), const, pipeline_mode=resident),
        pl.BlockSpec((SUBLANES, GB_W), const),
        pl.BlockSpec((2 * N_HEADS, LANES), const),
    ]
    args = [x, w["g_pre_mix"], w["wu"], w["wqkv"], w["wz"], w["wg"], w["wba"], w["wbat"],
            w["pcol"], w["prow"]]
    tail_specs = [
        pl.BlockSpec((tm, 2 * D_MODEL), tok),
        pl.BlockSpec((tm, GB_W), tok),
        pl.BlockSpec((2 * N_HEADS, tm), tok_t),
    ]
    tail_shapes = [
        jax.ShapeDtypeStruct((n, 2 * D_MODEL), F32),
        jax.ShapeDtypeStruct((n, GB_W), F32),
        jax.ShapeDtypeStruct((2 * N_HEADS, n), F32),
    ]
    if fused:
        in_specs += [pl.BlockSpec((CONV_W, QKV_W), const),
                     pl.BlockSpec((1, CONV_W - 1, QKV_W), lambda s, t: (0, 0, 0))]
        args += [w["conv_w"], conv_prefix]
        out_specs = [
            pl.BlockSpec((tm, POOL_WIDTH), tok),
            pl.BlockSpec((tm, KEY_W), tok),
            pl.BlockSpec((tm, KEY_W), tok),
            pl.BlockSpec((tm, VAL_W), tok),
            pl.BlockSpec((tm, VAL_W), tok),
        ] + tail_specs + [pl.BlockSpec((1, CONV_W - 1, QKV_W), lambda s, t: (s, 0, 0))]
        out_shape = [
            jax.ShapeDtypeStruct((n, POOL_WIDTH), F32),
            jax.ShapeDtypeStruct((n, KEY_W), F32),
            jax.ShapeDtypeStruct((n, KEY_W), F32),
            jax.ShapeDtypeStruct((n, VAL_W), F32),
            jax.ShapeDtypeStruct((n, VAL_W), F32),
        ] + tail_shapes + [jax.ShapeDtypeStruct((n_seq, CONV_W - 1, QKV_W), F32)]
        body = functools.partial(_in_proj_conv_kernel, tm, n_tiles)
        scratch = [pltpu.VMEM((SUBLANES + tm, QKV_W), F32)]
        semantics = ("parallel", "arbitrary")
    else:
        out_specs = [
            pl.BlockSpec((tm, POOL_WIDTH), tok),
            pl.BlockSpec((tm, QKV_W), tok),
            pl.BlockSpec((tm, VAL_W), tok),
        ] + tail_specs
        out_shape = [
            jax.ShapeDtypeStruct((n, POOL_WIDTH), F32),
            jax.ShapeDtypeStruct((n, QKV_W), F32),
            jax.ShapeDtypeStruct((n, VAL_W), F32),
        ] + tail_shapes
        body = _in_proj_kernel
        scratch = []
        semantics = ("parallel",)
    return pl.pallas_call(
        body,
        grid=grid,
        in_specs=in_specs,
        out_specs=out_specs,
        out_shape=out_shape,
        scratch_shapes=scratch,
        compiler_params=pltpu.CompilerParams(
            dimension_semantics=semantics, vmem_limit_bytes=VMEM_LIMIT),
        name="in_proj_conv" if fused else "in_proj",
    )(*args)


def _mixer_kernel(nseq, seq_len, n_valid, pos0, n_steps, preconv, has_prev, *refs):
    refs = list(refs)
    if preconv:
        (u_ref, qn_ref, kn_ref, v_ref, z_ref, gbc_ref, gbr_ref, poolp_ref, s0_ref,
         onorm_ref, poolw_ref, pscale_ref) = refs[:12]
        rest = refs[12 + has_prev:]
        a_ref, d_ref, poolo_ref, so_ref, u_ext = rest
    else:
        (u_ref, qkv_ref, z_ref, gbc_ref, gbr_ref, convp_ref, poolp_ref, s0_ref,
         convw_ref, onorm_ref, poolw_ref, pscale_ref) = refs[:12]
        rest = refs[12 + has_prev:]
        a_ref, d_ref, convo_ref, poolo_ref, so_ref, qkv_ext, u_ext = rest
    t = pl.program_id(1)
    conv_base = SUBLANES
    pool_base = 2 * SUBLANES
    last = n_valid if n_valid < seq_len else seq_len

    if n_steps > 1:
        @pl.when(t > 0)
        def _():
            u_ext[0, pool_base - POOL_STATE:pool_base, :] = (
                u_ext[0, pool_base + seq_len - POOL_STATE:pool_base + seq_len, :])
            if not preconv:
                qkv_ext[0, conv_base - (CONV_W - 1):conv_base, :] = (
                    qkv_ext[0, conv_base + seq_len - (CONV_W - 1):conv_base + seq_len, :])

    @pl.when(t == 0)
    def _():
        for s in range(nseq):
            u_ext[s, pool_base - POOL_STATE:pool_base, :] = poolp_ref[s]
            if not preconv:
                qkv_ext[s, conv_base - (CONV_W - 1):conv_base, :] = convp_ref[s]
        if n_steps > 1:
            so_ref[...] = s0_ref[...]

    s_in = so_ref if n_steps > 1 else s0_ref

    for s in range(nseq):
        u_ext[s, pool_base:pool_base + seq_len, :] = u_ref[s * seq_len:(s + 1) * seq_len, :]
        poolo_ref[s] = u_ext[s, pool_base + last - POOL_STATE:pool_base + last, :]
        if not preconv:
            qkv_ext[s, conv_base:conv_base + seq_len, :] = qkv_ref[s * seq_len:(s + 1) * seq_len, :]
            convo_ref[s] = qkv_ext[s, conv_base + last - (CONV_W - 1):conv_base + last, :]

    for gi, win in enumerate(POOL_WINDOWS):
        lanes = slice(gi * POOL_GROUP_W, (gi + 1) * POOL_GROUP_W)
        if pos0 >= POOL_STATE:
            cnt = float(win)
        else:
            pos = lax.broadcasted_iota(jnp.int32, (seq_len, 1), 0) + (pos0 + 1) + t * seq_len
            cnt = jnp.minimum(pos, win).astype(F32)
        pieces = []
        for s in range(nseq):
            tok = u_ext[s, pool_base:pool_base + seq_len, lanes]
            tot = tok
            for j in range(1, win):
                tot = tot + u_ext[s, pool_base - j:pool_base - j + seq_len, lanes]
            pieces.append(tot / cnt - tok)
        dev = pieces[0] if nseq == 1 else jnp.concatenate(pieces, axis=0)
        a_ref[:, lanes] = _dot(dev, poolw_ref[gi]) * pscale_ref[:, lanes]

    gbc = gbc_ref[...]
    gbr = gbr_ref[...]
    if n_valid < seq_len:
        gbc = jnp.where(lax.broadcasted_iota(jnp.int32, gbc.shape, 0) < n_valid, gbc, 0.0)
        gbr = jnp.where(lax.broadcasted_iota(jnp.int32, gbr.shape, 1) < n_valid, gbr, 0.0)
    ri = lax.broadcasted_iota(jnp.int32, (SLAB, SLAB), 0)
    ci = lax.broadcasted_iota(jnp.int32, (SLAB, SLAB), 1)
    shift = seq_len.bit_length() - 1
    same = (ri >> shift) == (ci >> shift)
    m_le = same & (ci <= ri)
    m_lt = same & (ci < ri)
    gc_col = _dot_hi(m_le.astype(F32), gbc)
    gc_row = _dot_hi(gbr, (same & (ri <= ci)).astype(F32))
    gl_col = _dot_hi(same.astype(F32), gbc)

    heads = range(N_HEADS)
    eye = (ri == ci).astype(F32)
    narrow = (lambda a: a.astype(BF16)) if seq_len % (2 * SUBLANES) == 0 else (lambda a: a)

    def conv(lanes):
        pieces = [_causal_conv_silu(qkv_ext, s, conv_base, seq_len, lanes, convw_ref)
                  for s in range(nseq)]
        return pieces[0] if nseq == 1 else jnp.concatenate(pieces, axis=0)

    kq, k_op, v, beta, e_g, e_dec, carry, decay_le, decay_lb = [], [], [], [], [], [], [], [], []
    for h in heads:
        hl = slice(h * HEAD_K, (h + 1) * HEAD_K)
        if preconv:
            q, k = narrow(qn_ref[:, hl]), narrow(kn_ref[:, hl])
            v.append(v_ref[:, hl])
        else:
            q = narrow(_l2_normalize(conv(hl), HEAD_K ** -0.5))
            k = narrow(_l2_normalize(conv(slice(KEY_W + h * HEAD_K, KEY_W + (h + 1) * HEAD_K))))
            v.append(conv(slice(2 * KEY_W + h * HEAD_V, 2 * KEY_W + (h + 1) * HEAD_V)))
        k_op.append(k)
        kq.append(jnp.concatenate([k, q], axis=0))
        b = gbc[:, h:h + 1]
        g_c = gc_col[:, N_HEADS + h:N_HEADS + h + 1]
        g_r = gc_row[N_HEADS + h:N_HEADS + h + 1, :]
        g_l = gl_col[:, N_HEADS + h:N_HEADS + h + 1]
        d_le = jnp.exp(jnp.where(m_le, g_c - g_r, NEG_BIG))
        decay_le.append(d_le)
        decay_lb.append(jnp.where(m_lt, d_le, 0.0) * b)
        beta.append(b)
        e_g.append(jnp.exp(g_c))
        e_dec.append(jnp.exp(g_l - g_c))
        carry.append([jnp.exp(g_l[s * seq_len:s * seq_len + 1, :]) for s in range(nseq)])

    kk_qk = [_dot_nt(kq[h], k_op[h]) for h in heads]
    lmat = [kk_qk[h][:SLAB] * decay_lb[h] for h in heads]
    attn = [(kk_qk[h][SLAB:] * decay_le[h]).astype(BF16) for h in heads]

    k_s, q_s = [], []
    for h in heads:
        if nseq == 1:
            prod = _dot(kq[h], s_in[0, h])
            k_s.append(prod[:SLAB])
            q_s.append(prod[SLAB:])
        else:
            ks_parts, qs_parts = [], []
            for s in range(nseq):
                rows = slice(s * seq_len, (s + 1) * seq_len)
                pair_rows = jnp.concatenate([kq[h][rows], kq[h][SLAB + s * seq_len:SLAB + (s + 1) * seq_len]], axis=0)
                prod = _dot(pair_rows, s_in[s, h])
                ks_parts.append(prod[:seq_len])
                qs_parts.append(prod[seq_len:])
            k_s.append(jnp.concatenate(ks_parts, axis=0))
            q_s.append(jnp.concatenate(qs_parts, axis=0))
    rhs = [(beta[h] * (v[h] - e_g[h] * k_s[h])).astype(BF16) for h in heads]

    pair = (ri >> 1) == (ci >> 1)
    l_bf = [lmat[h].astype(BF16) for h in heads]
    inv = [eye - jnp.where(pair, lmat[h], 0.0) for h in heads]
    blk = 2
    while blk < seq_len:
        sh = blk.bit_length() - 1
        lower_left = ((ri >> (sh + 1)) == (ci >> (sh + 1))) & ((ri >> sh) != (ci >> sh))
        inv_bf = [inv[h].astype(BF16) for h in heads]
        half = [_dot(inv_bf[h], l_bf[h]).astype(BF16) for h in heads]
        inv = [inv[h] - jnp.where(lower_left, _dot(half[h], inv_bf[h]), 0.0) for h in heads]
        blk *= 2

    v_new = [_dot(inv[h], rhs[h]) for h in heads]
    v_att = [narrow(v_new[h]) for h in heads]
    v_dec = [narrow(v_new[h] * e_dec[h]) for h in heads]
    for h in heads:
        for s in range(nseq):
            rows = slice(s * seq_len, (s + 1) * seq_len)
            so_ref[s, h] = s_in[s, h] * carry[h][s] + _dot_tn(k_op[h][rows], v_dec[h][rows])
    for h in heads:
        o = e_g[h] * q_s[h] + _dot(attn[h], v_att[h])
        o = o * lax.rsqrt(jnp.mean(o * o, axis=-1, keepdims=True) + EPS) * onorm_ref[...]
        zl = slice(h * HEAD_V, (h + 1) * HEAD_V)
        d_ref[:, zl] = o * (z_ref[:, zl] if preconv else _silu(z_ref[:, zl]))


def _mixer(tokens, gbc, gbr, convp, poolp, s0, w, *, nseq, seq_len, n_valid, pos0,
           n_groups, n_steps, row_block0, shared_state, layer=None, so_prev=None):
    assert nseq * seq_len == SLAB and (nseq == 1 or n_steps == 1)
    preconv = convp is None
    tok = lambda gi, t: (row_block0 + gi * n_steps + t, 0)
    tok_t = lambda gi, t: (0, row_block0 + gi * n_steps + t)
    out_tok = lambda gi, t: (gi * n_steps + t, 0)
    st3 = (lambda gi, t: (0, 0, 0)) if shared_state else (lambda gi, t: (gi, 0, 0))
    c2 = lambda gi, t: (0, 0)
    c3 = lambda gi, t: (0, 0, 0)
    state_blk = (nseq, N_HEADS, HEAD_K, HEAD_V)
    if layer is None:
        s0_spec = pl.BlockSpec(state_blk, (lambda gi, t: (0, 0, 0, 0)) if shared_state
                               else (lambda gi, t: (gi, 0, 0, 0)))
        so_spec = pl.BlockSpec(state_blk, lambda gi, t: (gi, 0, 0, 0))
        so_shape = jax.ShapeDtypeStruct((n_groups * nseq,) + state_blk[1:], F32)
    else:
        assert not shared_state
        s0_spec = pl.BlockSpec((None,) + state_blk, lambda gi, t: (layer, gi, 0, 0, 0))
        so_spec = pl.BlockSpec((None,) + state_blk, lambda gi, t: (layer, gi, 0, 0, 0))
        so_shape = jax.ShapeDtypeStruct(s0.shape, F32)
    widths = [a.shape[1] for a in tokens]
    in_specs = [pl.BlockSpec((SLAB, wd), tok) for wd in widths]
    in_specs += [pl.BlockSpec((SLAB, GB_W), tok), pl.BlockSpec((2 * N_HEADS, SLAB), tok_t)]
    args = list(tokens) + [gbc, gbr]
    if not preconv:
        in_specs.append(pl.BlockSpec((nseq, CONV_W - 1, QKV_W), st3))
        args.append(convp)
    in_specs += [pl.BlockSpec((nseq, POOL_STATE, POOL_WIDTH), st3), s0_spec]
    args += [poolp, s0]
    if not preconv:
        in_specs.append(pl.BlockSpec((CONV_W, QKV_W), c2))
        args.append(w["conv_w"])
    in_specs += [pl.BlockSpec((1, HEAD_V), c2),
                 pl.BlockSpec((POOL_GROUPS, POOL_GROUP_W, POOL_GROUP_W), c3),
                 pl.BlockSpec((1, POOL_WIDTH), c2)]
    args += [w["o_norm_g"], w["pool_w"], w["pool_scale"]]
    n_seq_total = n_groups * nseq
    n_rows = n_groups * n_steps * SLAB
    out_specs = [pl.BlockSpec((SLAB, POOL_WIDTH), out_tok), pl.BlockSpec((SLAB, VAL_W), out_tok)]
    out_shape = [jax.ShapeDtypeStruct((n_rows, POOL_WIDTH), F32),
                 jax.ShapeDtypeStruct((n_rows, VAL_W), F32)]
    scratch = []
    if not preconv:
        out_specs.append(pl.BlockSpec((nseq, CONV_W - 1, QKV_W), lambda gi, t: (gi, 0, 0)))
        out_shape.append(jax.ShapeDtypeStruct((n_seq_total, CONV_W - 1, QKV_W), F32))
        scratch.append(pltpu.VMEM((nseq, SUBLANES + seq_len, QKV_W), F32))
    out_specs += [pl.BlockSpec((nseq, POOL_STATE, POOL_WIDTH), lambda gi, t: (gi, 0, 0)), so_spec]
    out_shape += [jax.ShapeDtypeStruct((n_seq_total, POOL_STATE, POOL_WIDTH), F32), so_shape]
    scratch.append(pltpu.VMEM((nseq, 2 * SUBLANES + seq_len, POOL_WIDTH), F32))
    aliases = {}
    if so_prev is not None:
        in_specs.append(pl.BlockSpec(memory_space=pl.ANY))
        args.append(so_prev)
        aliases = {len(args) - 1: len(out_shape) - 1}
    return pl.pallas_call(
        functools.partial(_mixer_kernel, nseq, seq_len, n_valid, pos0, n_steps, preconv,
                          int(so_prev is not None)),
        grid=(n_groups, n_steps),
        in_specs=in_specs,
        out_specs=out_specs,
        out_shape=out_shape,
        scratch_shapes=scratch,
        input_output_aliases=aliases,
        compiler_params=pltpu.CompilerParams(
            dimension_semantics=("parallel", "arbitrary"), vmem_limit_bytes=VMEM_LIMIT),
        name="mixer_%dx%d%s" % (nseq, seq_len, "_preconv" if preconv else ""),
    )(*args)


def _out_ffn_kernel(x_ref, a_ref, d_ref, gates_ref, wbp_ref, wbd_ref, wout_ref, wup_ref, wdown_ref,
                    gpm_ref, gpf_ref, gqf_ref, y_ref):
    br_pool = jnp.dot(a_ref[...].astype(BF16), wbp_ref[...], preferred_element_type=F32)
    br_delta = jnp.dot(d_ref[...].astype(BF16), wbd_ref[...], preferred_element_type=F32)
    gate_pool = _sigmoid(gates_ref[:, :D_MODEL])
    gate_delta = _sigmoid(gates_ref[:, D_MODEL:])
    merged = (gate_pool * br_pool + gate_delta * br_delta).astype(BF16)
    m = jnp.dot(merged, wout_ref[...], preferred_element_type=F32)
    x = x_ref[...] + _rms(m, gpm_ref[...])
    h2 = _rms(x, gpf_ref[...]).astype(BF16)
    up = jnp.dot(h2, wup_ref[...], preferred_element_type=F32)
    act = jnp.square(jnp.maximum(up, 0.0)).astype(BF16)
    f = jnp.dot(act, wdown_ref[...], preferred_element_type=F32)
    y_ref[...] = x + _rms(f, gqf_ref[...])


def _out_ffn(x, a, d, gates, w, tm):
    n = x.shape[0]
    assert n % tm == 0
    const = lambda i: (0, 0)
    tok = lambda i: (i, 0)
    resident = pl.Buffered(1)
    in_specs = [
        pl.BlockSpec((tm, D_MODEL), tok),
        pl.BlockSpec((tm, POOL_WIDTH), tok),
        pl.BlockSpec((tm, VAL_W), tok),
        pl.BlockSpec((tm, 2 * D_MODEL), tok),
        pl.BlockSpec((POOL_WIDTH, D_MODEL), const, pipeline_mode=resident),
        pl.BlockSpec((VAL_W, D_MODEL), const, pipeline_mode=resident),
        pl.BlockSpec((D_MODEL, D_MODEL), const, pipeline_mode=resident),
        pl.BlockSpec((D_MODEL, D_FF), const, pipeline_mode=resident),
        pl.BlockSpec((D_FF, D_MODEL), const, pipeline_mode=resident),
        pl.BlockSpec((1, D_MODEL), const),
        pl.BlockSpec((1, D_MODEL), const),
        pl.BlockSpec((1, D_MODEL), const),
    ]
    return pl.pallas_call(
        _out_ffn_kernel,
        grid=(n // tm,),
        in_specs=in_specs,
        out_specs=pl.BlockSpec((tm, D_MODEL), tok),
        out_shape=jax.ShapeDtypeStruct((n, D_MODEL), F32),
        compiler_params=pltpu.CompilerParams(
            dimension_semantics=("parallel",), vmem_limit_bytes=VMEM_LIMIT),
        name="out_ffn",
    )(x, a, d, gates, w["wbp"], w["wbd"], w["wout"], w["wup"], w["wdown"],
      w["g_post_mix"], w["g_pre_ffn"], w["g_post_ffn"])


def _layer_weights(l, g_pre_mix, w_in, conv_w, a_log, dt_bias, o_norm_g, pool_w, pool_scale,
                   w_branch_pool, w_branch_delta, w_out, g_post_mix, g_pre_ffn, w_up, w_down,
                   g_post_ffn):
    wi = w_in[l]
    o_qkv = POOL_WIDTH
    o_b = o_qkv + QKV_W
    o_z = o_b + 2 * N_HEADS
    o_g = o_z + VAL_W
    wba = wi[:, o_b:o_z]
    zeros8 = jnp.zeros((N_HEADS,), F32)
    a_pad = jnp.concatenate([zeros8, a_log[l]])
    dt_pad = jnp.concatenate([zeros8, dt_bias[l]])
    pcol = jnp.zeros((SUBLANES, GB_W), F32).at[0, :2 * N_HEADS].set(a_pad).at[1, :2 * N_HEADS].set(dt_pad)
    prow = jnp.zeros((2 * N_HEADS, LANES), F32).at[:, 0].set(a_pad).at[:, 1].set(dt_pad)
    return {
        "g_pre_mix": g_pre_mix[l][None],
        "wu": wi[:, :o_qkv].astype(BF16),
        "wqkv": wi[:, o_qkv:o_b].astype(BF16),
        "wz": wi[:, o_z:o_g].astype(BF16),
        "wg": wi[:, o_g:].astype(BF16),
        "wba": jnp.pad(wba, ((0, 0), (0, GB_W - 2 * N_HEADS))).astype(BF16),
        "wbat": wba.T.astype(BF16),
        "pcol": pcol,
        "prow": prow,
        "conv_w": conv_w[l],
        "o_norm_g": o_norm_g[l][None],
        "pool_w": pool_w[l].astype(BF16),
        "pool_scale": pool_scale[l][None],
        "wbp": w_branch_pool[l].astype(BF16),
        "wbd": w_branch_delta[l].astype(BF16),
        "wout": w_out[l].astype(BF16),
        "wup": w_up[l].astype(BF16),
        "wdown": w_down[l].astype(BF16),
        "g_post_mix": g_post_mix[l][None],
        "g_pre_ffn": g_pre_ffn[l][None],
        "g_post_ffn": g_post_ffn[l][None],
    }


def kernel(x_prompt, x_sample, state_conv, state_ssm, state_pool, meta_tokens, g_pre_mix, w_in, conv_w, a_log, dt_bias, o_norm_g, pool_w, pool_scale, w_branch_pool, w_branch_delta, w_out, g_post_mix, g_pre_ffn, w_up, w_down, g_post_ffn):
    bp, seq, _ = x_prompt.shape
    bs, dec_seq, _ = x_sample.shape
    depth = w_in.shape[0]
    n_prompt = bp * seq
    n_sample = bs * dec_seq
    assert seq % 256 == 0 and n_sample % SLAB == 0 and SLAB % dec_seq == 0 and N_META <= SLAB
    xp = x_prompt.reshape(n_prompt, D_MODEL)
    xs = jnp.concatenate([x_sample.reshape(n_sample, D_MODEL), meta_tokens.astype(F32),
                          jnp.zeros((SLAB - N_META, D_MODEL), F32)], axis=0)
    n_small = n_sample + SLAB
    small_tm = 384 if n_small % 384 == 0 else SLAB
    meta_block = n_sample // SLAB
    zero_conv = jnp.zeros((1, CONV_W - 1, QKV_W), F32)
    zero_pool = jnp.zeros((1, POOL_STATE, POOL_WIDTH), F32)
    zero_ssm = jnp.zeros((1, N_HEADS, HEAD_K, HEAD_V), F32)
    state_ssm = state_ssm.astype(F32)
    outs = [[] for _ in range(5)]
    ssm_s = None
    for l in range(depth):
        w = _layer_weights(l, g_pre_mix, w_in, conv_w, a_log, dt_bias, o_norm_g, pool_w, pool_scale,
                           w_branch_pool, w_branch_delta, w_out, g_post_mix, g_pre_ffn, w_up,
                           w_down, g_post_ffn)
        us, qkvs, zs, gatess, gbcs, gbrs = _in_proj(xs, w, small_tm)
        a_m, d_m, conv_m, pool_m, ssm_m = _mixer(
            (us, qkvs, zs), gbcs, gbrs, zero_conv, zero_pool, zero_ssm, w,
            nseq=1, seq_len=SLAB, n_valid=N_META, pos0=0, n_groups=1, n_steps=1,
            row_block0=meta_block, shared_state=True)
        a_s, d_s, conv_s, pool_s, ssm_s = _mixer(
            (us, qkvs, zs), gbcs, gbrs, state_conv[l], state_pool[l], state_ssm, w,
            nseq=SLAB // dec_seq, seq_len=dec_seq, n_valid=dec_seq, pos0=PAST_LEN,
            n_groups=n_sample // SLAB, n_steps=1, row_block0=0, shared_state=False,
            layer=l, so_prev=ssm_s)
        up, qnp, knp, vp, szp, gatesp, gbcp, gbrp, conv_p = _in_proj(
            xp, w, 256, conv_prefix=conv_m, n_seq=bp)
        a_p, d_p, pool_p, ssm_p = _mixer(
            (up, qnp, knp, vp, szp), gbcp, gbrp, None, pool_m, ssm_m, w,
            nseq=1, seq_len=SLAB, n_valid=SLAB, pos0=N_META, n_groups=bp, n_steps=seq // SLAB,
            row_block0=0, shared_state=True)
        xp = _out_ffn(xp, a_p, d_p, gatesp, w, 256)
        xs = _out_ffn(xs, jnp.concatenate([a_s, a_m], axis=0), jnp.concatenate([d_s, d_m], axis=0),
                      gatess, w, small_tm)
        for acc, val in zip(outs, (conv_p, ssm_p, pool_p, conv_s, pool_s)):
            acc.append(val)
    y_prompt = xp.reshape(bp, seq, D_MODEL)
    y_sample = xs[:n_sample].reshape(bs, dec_seq, D_MODEL)
    conv_p, ssm_p, pool_p, conv_s, pool_s = (jnp.stack(o) for o in outs)
    return (y_prompt, y_sample, conv_p, ssm_p, pool_p, conv_s, ssm_s, pool_s)
```

```python
import functools

import jax
import jax.numpy as jnp
from jax import lax
from jax.experimental import pallas as pl
from jax.experimental.pallas import tpu as pltpu

D_MODEL = 1024
N_META = 16
POOL_GROUPS = 4
POOL_GROUP_W = 128
POOL_WIDTH = 512
POOL_WINDOWS = (2, 4, 8, 16)
POOL_STATE = 15
HEAD_K = 128
HEAD_V = 128
N_HEADS = 8
KEY_W = 1024
VAL_W = 1024
QKV_W = 3072
CONV_W = 4
D_FF = 4096
PAST_LEN = 16384
EPS = 1e-6

LANES = 128
SUBLANES = 8
SLAB = 128
PROJ_CHUNK = 256
GB_W = 128
VMEM_LIMIT = 56 * 1024 * 1024

F32 = jnp.float32
BF16 = jnp.bfloat16
HI = lax.Precision.HIGHEST
NEG_BIG = -1e30
NEG_LOG2_E = -1.4426950408889634


def _dot(a, b):
    return jnp.dot(a.astype(BF16), b.astype(BF16), preferred_element_type=F32)


def _dot_nt(a, b):
    return lax.dot_general(a.astype(BF16), b.astype(BF16), (((1,), (1,)), ((), ())),
                           preferred_element_type=F32)


def _dot_tn(a, b):
    return lax.dot_general(a.astype(BF16), b.astype(BF16), (((0,), (0,)), ((), ())),
                           preferred_element_type=F32)


def _dot_hi(a, b):
    return jnp.dot(a, b, precision=HI, preferred_element_type=F32)


def _rms(x, g):
    return x * lax.rsqrt(jnp.mean(x * x, axis=-1, keepdims=True) + EPS) * g


def _sigmoid(x):
    return 1.0 / (1.0 + jnp.exp2(x * NEG_LOG2_E))


def _silu(x):
    return x * _sigmoid(x)


def _softplus(x):
    return jnp.maximum(x, 0.0) + jnp.log1p(jnp.exp(-jnp.abs(x)))


def _l2_normalize(x, scale=1.0):
    return x * (lax.rsqrt(jnp.sum(x * x, axis=-1, keepdims=True) + EPS) * scale)


def _causal_conv_silu(ext_ref, seq, row0, n_rows, lanes, convw_ref):
    acc = None
    for j in range(CONV_W):
        lo = row0 - (CONV_W - 1) + j
        rows = ext_ref[lo:lo + n_rows, lanes] if seq is None else ext_ref[seq, lo:lo + n_rows, lanes]
        term = rows * convw_ref[j:j + 1, lanes]
        acc = term if acc is None else acc + term
    return _silu(acc)


def _beta_decay_cols(ba, pcol_ref):
    lane = lax.broadcasted_iota(jnp.int32, ba.shape, 1)
    g_col = -jnp.exp(pcol_ref[0:1, :]) * _softplus(ba + pcol_ref[1:2, :])
    return jnp.where(lane < N_HEADS, _sigmoid(ba), jnp.where(lane < 2 * N_HEADS, g_col, 0.0))


def _beta_decay_rows(bat, prow_ref):
    row = lax.broadcasted_iota(jnp.int32, bat.shape, 0)
    g_row = -jnp.exp(prow_ref[:, 0:1]) * _softplus(bat + prow_ref[:, 1:2])
    return jnp.where(row < N_HEADS, _sigmoid(bat), g_row)


def _in_proj_kernel(x_ref, g_ref, wu_ref, wqkv_ref, wz_ref, wg_ref, wba_ref, wbat_ref,
                    pcol_ref, prow_ref,
                    u_ref, qkv_ref, z_ref, gates_ref, gbc_ref, gbr_ref):
    h = _rms(x_ref[...], g_ref[...]).astype(BF16)
    u_ref[...] = jnp.dot(h, wu_ref[...], preferred_element_type=F32)
    qkv_ref[...] = jnp.dot(h, wqkv_ref[...], preferred_element_type=F32)
    z_ref[...] = jnp.dot(h, wz_ref[...], preferred_element_type=F32)
    gates_ref[...] = jnp.dot(h, wg_ref[...], preferred_element_type=F32)
    gbc_ref[...] = _beta_decay_cols(jnp.dot(h, wba_ref[...], preferred_element_type=F32), pcol_ref)
    bat = lax.dot_general(wbat_ref[...], h, (((1,), (1,)), ((), ())), preferred_element_type=F32)
    gbr_ref[...] = _beta_decay_rows(bat, prow_ref)


def _in_proj_conv_kernel(tm, n_tiles,
                         x_ref, g_ref, wu_ref, wqkv_ref, wz_ref, wg_ref, wba_ref, wbat_ref,
                         pcol_ref, prow_ref, convw_ref, convp_ref,
                         u_ref, qn_ref, kn_ref, v_ref, z_ref, gates_ref, gbc_ref, gbr_ref, convo_ref,
                         ext):
    t = pl.program_id(1)
    base = SUBLANES
    if n_tiles > 1:
        @pl.when(t > 0)
        def _():
            ext[base - (CONV_W - 1):base, :] = ext[base + tm - (CONV_W - 1):base + tm, :]

    @pl.when(t == 0)
    def _():
        ext[base - (CONV_W - 1):base, :] = convp_ref[0]

    h = _rms(x_ref[...], g_ref[...]).astype(BF16)
    u_ref[...] = jnp.dot(h, wu_ref[...], preferred_element_type=F32)
    ext[base:base + tm, :] = jnp.dot(h, wqkv_ref[...], preferred_element_type=F32)
    convo_ref[0] = ext[base + tm - (CONV_W - 1):base + tm, :]
    n_groups = QKV_W // LANES
    chunks = ([(z_ref, wz_ref, c0) for c0 in range(0, VAL_W, PROJ_CHUNK)]
              + [(gates_ref, wg_ref, c0) for c0 in range(0, 2 * D_MODEL, PROJ_CHUNK)])
    assert n_groups % len(chunks) == 0
    every = n_groups // len(chunks)
    for c in range(n_groups):
        lanes = slice(c * LANES, (c + 1) * LANES)
        head_lanes = slice((c % N_HEADS) * LANES, (c % N_HEADS + 1) * LANES)
        y = _causal_conv_silu(ext, None, base, tm, lanes, convw_ref)
        if c < N_HEADS:
            qn_ref[:, head_lanes] = _l2_normalize(y, HEAD_K ** -0.5)
        elif c < 2 * N_HEADS:
            kn_ref[:, head_lanes] = _l2_normalize(y)
        else:
            v_ref[:, head_lanes] = y
        if c % every == 0:
            out_ref, w_ref, c0 = chunks[c // every]
            out_ref[:, c0:c0 + PROJ_CHUNK] = jnp.dot(h, w_ref[:, c0:c0 + PROJ_CHUNK],
                                                     preferred_element_type=F32)
    gbc_ref[...] = _beta_decay_cols(jnp.dot(h, wba_ref[...], preferred_element_type=F32), pcol_ref)
    bat = lax.dot_general(wbat_ref[...], h, (((1,), (1,)), ((), ())), preferred_element_type=F32)
    gbr_ref[...] = _beta_decay_rows(bat, prow_ref)


def _in_proj(x, w, tm, conv_prefix=None, n_seq=1):
    n = x.shape[0]
    fused = conv_prefix is not None
    assert n % (tm * n_seq) == 0
    n_tiles = n // (tm * n_seq)
    if fused:
        grid = (n_seq, n_tiles)
        const = lambda s, t: (0, 0)
        tok = lambda s, t: (s * n_tiles + t, 0)
        tok_t = lambda s, t: (0, s * n_tiles + t)
    else:
        grid = (n_tiles,)
        const = lambda i: (0, 0)
        tok = lambda i: (i, 0)
        tok_t = lambda i: (0, i)
    resident = pl.Buffered(1)
    in_specs = [
        pl.BlockSpec((tm, D_MODEL), tok),
        pl.BlockSpec((1, D_MODEL), const),
        pl.BlockSpec((D_MODEL, POOL_WIDTH), const, pipeline_mode=resident),
        pl.BlockSpec((D_MODEL, QKV_W), const, pipeline_mode=resident),
        pl.BlockSpec((D_MODEL, VAL_W), const, pipeline_mode=resident),
        pl.BlockSpec((D_MODEL, 2 * D_MODEL), const, pipeline_mode=resident),
        pl.BlockSpec((D_MODEL, GB_W), const, pipeline_mode=resident),
        pl.BlockSpec((2 * N_HEADS, D_MODEL), const, pipeline_mode=resident),
        pl.BlockSpec((SUBLANES, GB_W), const),
        pl.BlockSpec((2 * N_HEADS, LANES), const),
    ]
    args = [x, w["g_pre_mix"], w["wu"], w["wqkv"], w["wz"], w["wg"], w["wba"], w["wbat"],
            w["pcol"], w["prow"]]
    tail_specs = [
        pl.BlockSpec((tm, 2 * D_MODEL), tok),
        pl.BlockSpec((tm, GB_W), tok),
        pl.BlockSpec((2 * N_HEADS, tm), tok_t),
    ]
    tail_shapes = [
        jax.ShapeDtypeStruct((n, 2 * D_MODEL), F32),
        jax.ShapeDtypeStruct((n, GB_W), F32),
        jax.ShapeDtypeStruct((2 * N_HEADS, n), F32),
    ]
    if fused:
        in_specs += [pl.BlockSpec((CONV_W, QKV_W), const),
                     pl.BlockSpec((1, CONV_W - 1, QKV_W), lambda s, t: (0, 0, 0))]
        args += [w["conv_w"], conv_prefix]
        out_specs = [
            pl.BlockSpec((tm, POOL_WIDTH), tok),
            pl.BlockSpec((tm, KEY_W), tok),
            pl.BlockSpec((tm, KEY_W), tok),
            pl.BlockSpec((tm, VAL_W), tok),
            pl.BlockSpec((tm, VAL_W), tok),
        ] + tail_specs + [pl.BlockSpec((1, CONV_W - 1, QKV_W), lambda s, t: (s, 0, 0))]
        out_shape = [
            jax.ShapeDtypeStruct((n, POOL_WIDTH), F32),
            jax.ShapeDtypeStruct((n, KEY_W), F32),
            jax.ShapeDtypeStruct((n, KEY_W), F32),
            jax.ShapeDtypeStruct((n, VAL_W), F32),
            jax.ShapeDtypeStruct((n, VAL_W), F32),
        ] + tail_shapes + [jax.ShapeDtypeStruct((n_seq, CONV_W - 1, QKV_W), F32)]
        body = functools.partial(_in_proj_conv_kernel, tm, n_tiles)
        scratch = [pltpu.VMEM((SUBLANES + tm, QKV_W), F32)]
        semantics = ("parallel", "arbitrary")
    else:
        out_specs = [
            pl.BlockSpec((tm, POOL_WIDTH), tok),
            pl.BlockSpec((tm, QKV_W), tok),
            pl.BlockSpec((tm, VAL_W), tok),
        ] + tail_specs
        out_shape = [
            jax.ShapeDtypeStruct((n, POOL_WIDTH), F32),
            jax.ShapeDtypeStruct((n, QKV_W), F32),
            jax.ShapeDtypeStruct((n, VAL_W), F32),
        ] + tail_shapes
        body = _in_proj_kernel
        scratch = []
        semantics = ("parallel",)
    return pl.pallas_call(
        body,
        grid=grid,
        in_specs=in_specs,
        out_specs=out_specs,
        out_shape=out_shape,
        scratch_shapes=scratch,
        compiler_params=pltpu.CompilerParams(
            dimension_semantics=semantics, vmem_limit_bytes=VMEM_LIMIT),
        name="in_proj_conv" if fused else "in_proj",
    )(*args)


def _mixer_kernel(nseq, seq_len, n_valid, pos0, n_steps, preconv, n_passthrough, *refs):
    refs = list(refs)
    rest = refs[11 + n_passthrough:]
    if preconv:
        (u_ref, qn_ref, kn_ref, v_ref, gbc_ref, gbr_ref, poolp_ref, s0_ref,
         onorm_ref, poolw_ref, pscale_ref) = refs[:11]
        a_ref, d_ref, poolo_ref, so_ref, u_ext = rest
    else:
        (u_ref, qkv_ref, gbc_ref, gbr_ref, convp_ref, poolp_ref, s0_ref,
         convw_ref, onorm_ref, poolw_ref, pscale_ref) = refs[:11]
        a_ref, d_ref, convo_ref, poolo_ref, so_ref, qkv_ext, u_ext = rest
    t = pl.program_id(1)
    conv_base = SUBLANES
    pool_base = 2 * SUBLANES
    last = n_valid if n_valid < seq_len else seq_len

    if n_steps > 1:
        @pl.when(t > 0)
        def _():
            u_ext[0, pool_base - POOL_STATE:pool_base, :] = (
                u_ext[0, pool_base + seq_len - POOL_STATE:pool_base + seq_len, :])
            if not preconv:
                qkv_ext[0, conv_base - (CONV_W - 1):conv_base, :] = (
                    qkv_ext[0, conv_base + seq_len - (CONV_W - 1):conv_base + seq_len, :])

    @pl.when(t == 0)
    def _():
        for s in range(nseq):
            u_ext[s, pool_base - POOL_STATE:pool_base, :] = poolp_ref[s]
            if not preconv:
                qkv_ext[s, conv_base - (CONV_W - 1):conv_base, :] = convp_ref[s]
        if n_steps > 1:
            so_ref[...] = s0_ref[...]

    s_in = so_ref if n_steps > 1 else s0_ref

    for s in range(nseq):
        u_ext[s, pool_base:pool_base + seq_len, :] = u_ref[s * seq_len:(s + 1) * seq_len, :]
        poolo_ref[s] = u_ext[s, pool_base + last - POOL_STATE:pool_base + last, :]
        if not preconv:
            qkv_ext[s, conv_base:conv_base + seq_len, :] = qkv_ref[s * seq_len:(s + 1) * seq_len, :]
            convo_ref[s] = qkv_ext[s, conv_base + last - (CONV_W - 1):conv_base + last, :]

    for gi, win in enumerate(POOL_WINDOWS):
        lanes = slice(gi * POOL_GROUP_W, (gi + 1) * POOL_GROUP_W)
        if pos0 >= POOL_STATE:
            cnt = float(win)
        else:
            pos = lax.broadcasted_iota(jnp.int32, (seq_len, 1), 0) + (pos0 + 1) + t * seq_len
            cnt = jnp.minimum(pos, win).astype(F32)
        pieces = []
        for s in range(nseq):
            tok = u_ext[s, pool_base:pool_base + seq_len, lanes]
            tot = tok
            for j in range(1, win):
                tot = tot + u_ext[s, pool_base - j:pool_base - j + seq_len, lanes]
            pieces.append(tot / cnt - tok)
        dev = pieces[0] if nseq == 1 else jnp.concatenate(pieces, axis=0)
        a_ref[:, lanes] = _dot(dev, poolw_ref[gi]) * pscale_ref[:, lanes]

    gbc = gbc_ref[...]
    gbr = gbr_ref[...]
    if n_valid < seq_len:
        gbc = jnp.where(lax.broadcasted_iota(jnp.int32, gbc.shape, 0) < n_valid, gbc, 0.0)
        gbr = jnp.where(lax.broadcasted_iota(jnp.int32, gbr.shape, 1) < n_valid, gbr, 0.0)
    ri = lax.broadcasted_iota(jnp.int32, (SLAB, SLAB), 0)
    ci = lax.broadcasted_iota(jnp.int32, (SLAB, SLAB), 1)
    shift = seq_len.bit_length() - 1
    same = (ri >> shift) == (ci >> shift)
    m_le = same & (ci <= ri)
    m_lt = same & (ci < ri)
    gc_col = _dot_hi(m_le.astype(F32), gbc)
    gc_row = _dot_hi(gbr, (same & (ri <= ci)).astype(F32))
    gl_col = _dot_hi(same.astype(F32), gbc)

    heads = range(N_HEADS)
    eye = (ri == ci).astype(F32)
    narrow = (lambda a: a.astype(BF16)) if seq_len % (2 * SUBLANES) == 0 else (lambda a: a)

    def conv(lanes):
        pieces = [_causal_conv_silu(qkv_ext, s, conv_base, seq_len, lanes, convw_ref)
                  for s in range(nseq)]
        return pieces[0] if nseq == 1 else jnp.concatenate(pieces, axis=0)

    kq, k_op, v, beta, e_g, e_dec, carry, decay_le, decay_lb = [], [], [], [], [], [], [], [], []
    for h in heads:
        hl = slice(h * HEAD_K, (h + 1) * HEAD_K)
        if preconv:
            q, k = narrow(qn_ref[:, hl]), narrow(kn_ref[:, hl])
            v.append(v_ref[:, hl])
        else:
            q = narrow(_l2_normalize(conv(hl), HEAD_K ** -0.5))
            k = narrow(_l2_normalize(conv(slice(KEY_W + h * HEAD_K, KEY_W + (h + 1) * HEAD_K))))
            v.append(conv(slice(2 * KEY_W + h * HEAD_V, 2 * KEY_W + (h + 1) * HEAD_V)))
        k_op.append(k)
        kq.append(jnp.concatenate([k, q], axis=0))
        b = gbc[:, h:h + 1]
        g_c = gc_col[:, N_HEADS + h:N_HEADS + h + 1]
        g_r = gc_row[N_HEADS + h:N_HEADS + h + 1, :]
        g_l = gl_col[:, N_HEADS + h:N_HEADS + h + 1]
        d_le = jnp.exp(jnp.where(m_le, g_c - g_r, NEG_BIG))
        decay_le.append(d_le)
        decay_lb.append(jnp.where(m_lt, d_le, 0.0) * b)
        beta.append(b)
        e_g.append(jnp.exp(g_c))
        e_dec.append(jnp.exp(g_l - g_c))
        carry.append([jnp.exp(g_l[s * seq_len:s * seq_len + 1, :]) for s in range(nseq)])

    kk_qk = [_dot_nt(kq[h], k_op[h]) for h in heads]
    lmat = [kk_qk[h][:SLAB] * decay_lb[h] for h in heads]
    attn = [(kk_qk[h][SLAB:] * decay_le[h]).astype(BF16) for h in heads]

    k_s, q_s = [], []
    for h in heads:
        if nseq == 1:
            prod = _dot(kq[h], s_in[0, h])
            k_s.append(prod[:SLAB])
            q_s.append(prod[SLAB:])
        else:
            ks_parts, qs_parts = [], []
            for s in range(nseq):
                rows = slice(s * seq_len, (s + 1) * seq_len)
                pair_rows = jnp.concatenate([kq[h][rows], kq[h][SLAB + s * seq_len:SLAB + (s + 1) * seq_len]], axis=0)
                prod = _dot(pair_rows, s_in[s, h])
                ks_parts.append(prod[:seq_len])
                qs_parts.append(prod[seq_len:])
            k_s.append(jnp.concatenate(ks_parts, axis=0))
            q_s.append(jnp.concatenate(qs_parts, axis=0))
    rhs = [(beta[h] * (v[h] - e_g[h] * k_s[h])).astype(BF16) for h in heads]

    pair = (ri >> 1) == (ci >> 1)
    l_bf = [lmat[h].astype(BF16) for h in heads]
    inv = [eye - jnp.where(pair, lmat[h], 0.0) for h in heads]
    blk = 2
    while blk < seq_len:
        sh = blk.bit_length() - 1
        lower_left = ((ri >> (sh + 1)) == (ci >> (sh + 1))) & ((ri >> sh) != (ci >> sh))
        inv_bf = [inv[h].astype(BF16) for h in heads]
        half = [_dot(inv_bf[h], l_bf[h]).astype(BF16) for h in heads]
        inv = [inv[h] - jnp.where(lower_left, _dot(half[h], inv_bf[h]), 0.0) for h in heads]
        blk *= 2

    v_new = [_dot(inv[h], rhs[h]) for h in heads]
    v_att = [narrow(v_new[h]) for h in heads]
    v_dec = [narrow(v_new[h] * e_dec[h]) for h in heads]
    for h in heads:
        for s in range(nseq):
            rows = slice(s * seq_len, (s + 1) * seq_len)
            so_ref[s, h] = s_in[s, h] * carry[h][s] + _dot_tn(k_op[h][rows], v_dec[h][rows])
    for h in heads:
        o = e_g[h] * q_s[h] + _dot(attn[h], v_att[h])
        d_ref[:, h * HEAD_V:(h + 1) * HEAD_V] = (
            o * lax.rsqrt(jnp.mean(o * o, axis=-1, keepdims=True) + EPS) * onorm_ref[...])


def _mixer(tokens, gbc, gbr, convp, poolp, s0, w, *, nseq, seq_len, n_valid, pos0,
           n_groups, n_steps, row_block0, shared_state, layer=None, prev_states=None):
    assert nseq * seq_len == SLAB and (nseq == 1 or n_steps == 1)
    assert layer is None or not shared_state
    preconv = convp is None
    n_seq_total = n_groups * nseq
    tok = lambda gi, t: (row_block0 + gi * n_steps + t, 0)
    tok_t = lambda gi, t: (0, row_block0 + gi * n_steps + t)
    out_tok = lambda gi, t: (gi * n_steps + t, 0)
    c2 = lambda gi, t: (0, 0)
    c3 = lambda gi, t: (0, 0, 0)

    def state_specs(per_seq, stacked_like):
        zeros = (0,) * len(per_seq)
        blk = (nseq,) + per_seq
        if layer is not None:
            spec = pl.BlockSpec((None,) + blk, lambda gi, t: (layer, gi) + zeros)
            return spec, spec, jax.ShapeDtypeStruct(stacked_like.shape, F32)
        in_spec = pl.BlockSpec(blk, (lambda gi, t: (0,) + zeros) if shared_state
                               else (lambda gi, t: (gi,) + zeros))
        out_spec = pl.BlockSpec(blk, lambda gi, t: (gi,) + zeros)
        return in_spec, out_spec, jax.ShapeDtypeStruct((n_seq_total,) + per_seq, F32)

    pool_in, pool_out, pool_shape = state_specs((POOL_STATE, POOL_WIDTH), poolp)
    s0_spec, so_spec, so_shape = state_specs((N_HEADS, HEAD_K, HEAD_V), s0)
    widths = [a.shape[1] for a in tokens]
    in_specs = [pl.BlockSpec((SLAB, wd), tok) for wd in widths]
    in_specs += [pl.BlockSpec((SLAB, GB_W), tok), pl.BlockSpec((2 * N_HEADS, SLAB), tok_t)]
    args = list(tokens) + [gbc, gbr]
    if not preconv:
        conv_in, conv_out, conv_shape = state_specs((CONV_W - 1, QKV_W), convp)
        in_specs.append(conv_in)
        args.append(convp)
    in_specs += [pool_in, s0_spec]
    args += [poolp, s0]
    if not preconv:
        in_specs.append(pl.BlockSpec((CONV_W, QKV_W), c2))
        args.append(w["conv_w"])
    in_specs += [pl.BlockSpec((1, HEAD_V), c2),
                 pl.BlockSpec((POOL_GROUPS, POOL_GROUP_W, POOL_GROUP_W), c3),
                 pl.BlockSpec((1, POOL_WIDTH), c2)]
    args += [w["o_norm_g"], w["pool_w"], w["pool_scale"]]
    n_rows = n_groups * n_steps * SLAB
    out_specs = [pl.BlockSpec((SLAB, POOL_WIDTH), out_tok), pl.BlockSpec((SLAB, VAL_W), out_tok)]
    out_shape = [jax.ShapeDtypeStruct((n_rows, POOL_WIDTH), F32),
                 jax.ShapeDtypeStruct((n_rows, VAL_W), F32)]
    scratch = []
    if not preconv:
        out_specs.append(conv_out)
        out_shape.append(conv_shape)
        scratch.append(pltpu.VMEM((nseq, SUBLANES + seq_len, QKV_W), F32))
    out_specs += [pool_out, so_spec]
    out_shape += [pool_shape, so_shape]
    scratch.append(pltpu.VMEM((nseq, 2 * SUBLANES + seq_len, POOL_WIDTH), F32))
    aliases = {}
    n_passthrough = 0
    if prev_states is not None:
        n_passthrough = len(prev_states)
        first_state_out = len(out_shape) - n_passthrough
        for i, prev in enumerate(prev_states):
            in_specs.append(pl.BlockSpec(memory_space=pl.ANY))
            args.append(prev)
            aliases[len(args) - 1] = first_state_out + i
    return pl.pallas_call(
        functools.partial(_mixer_kernel, nseq, seq_len, n_valid, pos0, n_steps, preconv,
                          n_passthrough),
        grid=(n_groups, n_steps),
        in_specs=in_specs,
        out_specs=out_specs,
        out_shape=out_shape,
        scratch_shapes=scratch,
        input_output_aliases=aliases,
        compiler_params=pltpu.CompilerParams(
            dimension_semantics=("parallel", "arbitrary"), vmem_limit_bytes=VMEM_LIMIT),
        name="mixer_%dx%d%s" % (nseq, seq_len, "_preconv" if preconv else ""),
    )(*args)


def _out_ffn_kernel(x_ref, a_ref, d_ref, z_ref, gates_ref, wbp_ref, wbd_ref, wout_ref, wup_ref,
                    wdown_ref, gpm_ref, gpf_ref, gqf_ref, y_ref):
    br_pool = jnp.dot(a_ref[...].astype(BF16), wbp_ref[...], preferred_element_type=F32)
    delta = (d_ref[...] * _silu(z_ref[...])).astype(BF16)
    br_delta = jnp.dot(delta, wbd_ref[...], preferred_element_type=F32)
    gate_pool = _sigmoid(gates_ref[:, :D_MODEL])
    gate_delta = _sigmoid(gates_ref[:, D_MODEL:])
    merged = (gate_pool * br_pool + gate_delta * br_delta).astype(BF16)
    m = jnp.dot(merged, wout_ref[...], preferred_element_type=F32)
    x = x_ref[...] + _rms(m, gpm_ref[...])
    h2 = _rms(x, gpf_ref[...]).astype(BF16)
    up = jnp.dot(h2, wup_ref[...], preferred_element_type=F32)
    act = jnp.square(jnp.maximum(up, 0.0)).astype(BF16)
    f = jnp.dot(act, wdown_ref[...], preferred_element_type=F32)
    y_ref[...] = x + _rms(f, gqf_ref[...])


def _out_ffn(x, a, d, z, gates, w, tm):
    n = x.shape[0]
    assert n % tm == 0
    const = lambda i: (0, 0)
    tok = lambda i: (i, 0)
    resident = pl.Buffered(1)
    in_specs = [
        pl.BlockSpec((tm, D_MODEL), tok),
        pl.BlockSpec((tm, POOL_WIDTH), tok),
        pl.BlockSpec((tm, VAL_W), tok),
        pl.BlockSpec((tm, VAL_W), tok),
        pl.BlockSpec((tm, 2 * D_MODEL), tok),
        pl.BlockSpec((POOL_WIDTH, D_MODEL), const, pipeline_mode=resident),
        pl.BlockSpec((VAL_W, D_MODEL), const, pipeline_mode=resident),
        pl.BlockSpec((D_MODEL, D_MODEL), const, pipeline_mode=resident),
        pl.BlockSpec((D_MODEL, D_FF), const, pipeline_mode=resident),
        pl.BlockSpec((D_FF, D_MODEL), const, pipeline_mode=resident),
        pl.BlockSpec((1, D_MODEL), const),
        pl.BlockSpec((1, D_MODEL), const),
        pl.BlockSpec((1, D_MODEL), const),
    ]
    return pl.pallas_call(
        _out_ffn_kernel,
        grid=(n // tm,),
        in_specs=in_specs,
        out_specs=pl.BlockSpec((tm, D_MODEL), tok),
        out_shape=jax.ShapeDtypeStruct((n, D_MODEL), F32),
        compiler_params=pltpu.CompilerParams(
            dimension_semantics=("parallel",), vmem_limit_bytes=VMEM_LIMIT),
        name="out_ffn",
    )(x, a, d, z, gates, w["wbp"], w["wbd"], w["wout"], w["wup"], w["wdown"],
      w["g_post_mix"], w["g_pre_ffn"], w["g_post_ffn"])


def _layer_weights(l, g_pre_mix, w_in, conv_w, a_log, dt_bias, o_norm_g, pool_w, pool_scale,
                   w_branch_pool, w_branch_delta, w_out, g_post_mix, g_pre_ffn, w_up, w_down,
                   g_post_ffn):
    wi = w_in[l]
    o_qkv = POOL_WIDTH
    o_b = o_qkv + QKV_W
    o_z = o_b + 2 * N_HEADS
    o_g = o_z + VAL_W
    wba = lax.optimization_barrier(wi[:, o_b:o_z])
    zeros8 = jnp.zeros((N_HEADS,), F32)
    a_pad = jnp.concatenate([zeros8, a_log[l]])
    dt_pad = jnp.concatenate([zeros8, dt_bias[l]])
    pcol = jnp.zeros((SUBLANES, GB_W), F32).at[0, :2 * N_HEADS].set(a_pad).at[1, :2 * N_HEADS].set(dt_pad)
    prow = jnp.zeros((2 * N_HEADS, LANES), F32).at[:, 0].set(a_pad).at[:, 1].set(dt_pad)
    return {
        "g_pre_mix": g_pre_mix[l][None],
        "wu": wi[:, :o_qkv].astype(BF16),
        "wqkv": wi[:, o_qkv:o_b].astype(BF16),
        "wz": wi[:, o_z:o_g].astype(BF16),
        "wg": wi[:, o_g:].astype(BF16),
        "wba": jnp.pad(wba, ((0, 0), (0, GB_W - 2 * N_HEADS))).astype(BF16),
        "wbat": wba.T.astype(BF16),
        "pcol": pcol,
        "prow": prow,
        "conv_w": conv_w[l],
        "o_norm_g": o_norm_g[l][None],
        "pool_w": pool_w[l].astype(BF16),
        "pool_scale": pool_scale[l][None],
        "wbp": w_branch_pool[l].astype(BF16),
        "wbd": w_branch_delta[l].astype(BF16),
        "wout": w_out[l].astype(BF16),
        "wup": w_up[l].astype(BF16),
        "wdown": w_down[l].astype(BF16),
        "g_post_mix": g_post_mix[l][None],
        "g_pre_ffn": g_pre_ffn[l][None],
        "g_post_ffn": g_post_ffn[l][None],
    }


def kernel(x_prompt, x_sample, state_conv, state_ssm, state_pool, meta_tokens, g_pre_mix, w_in, conv_w, a_log, dt_bias, o_norm_g, pool_w, pool_scale, w_branch_pool, w_branch_delta, w_out, g_post_mix, g_pre_ffn, w_up, w_down, g_post_ffn):
    bp, seq, _ = x_prompt.shape
    bs, dec_seq, _ = x_sample.shape
    depth = w_in.shape[0]
    n_prompt = bp * seq
    n_sample = bs * dec_seq
    assert seq % 256 == 0 and n_sample % SLAB == 0 and SLAB % dec_seq == 0 and N_META <= SLAB
    xp = x_prompt.reshape(n_prompt, D_MODEL)
    xs = jnp.concatenate([x_sample.reshape(n_sample, D_MODEL), meta_tokens.astype(F32),
                          jnp.zeros((SLAB - N_META, D_MODEL), F32)], axis=0)
    n_small = n_sample + SLAB
    small_tm = 384 if n_small % 384 == 0 else SLAB
    meta_block = n_sample // SLAB
    zero_conv = jnp.zeros((1, CONV_W - 1, QKV_W), F32)
    zero_pool = jnp.zeros((1, POOL_STATE, POOL_WIDTH), F32)
    zero_ssm = jnp.zeros((1, N_HEADS, HEAD_K, HEAD_V), F32)
    state_ssm = state_ssm.astype(F32)
    outs = [[] for _ in range(3)]
    sample_states = None
    for l in range(depth):
        w = _layer_weights(l, g_pre_mix, w_in, conv_w, a_log, dt_bias, o_norm_g, pool_w, pool_scale,
                           w_branch_pool, w_branch_delta, w_out, g_post_mix, g_pre_ffn, w_up,
                           w_down, g_post_ffn)
        us, qkvs, zs, gatess, gbcs, gbrs = _in_proj(xs, w, small_tm)
        a_m, d_m, conv_m, pool_m, ssm_m = _mixer(
            (us, qkvs), gbcs, gbrs, zero_conv, zero_pool, zero_ssm, w,
            nseq=1, seq_len=SLAB, n_valid=N_META, pos0=0, n_groups=1, n_steps=1,
            row_block0=meta_block, shared_state=True)
        a_s, d_s, *sample_states = _mixer(
            (us, qkvs), gbcs, gbrs, state_conv, state_pool, state_ssm, w,
            nseq=SLAB // dec_seq, seq_len=dec_seq, n_valid=dec_seq, pos0=PAST_LEN,
            n_groups=n_sample // SLAB, n_steps=1, row_block0=0, shared_state=False,
            layer=l, prev_states=sample_states)
        up, qnp, knp, vp, zp, gatesp, gbcp, gbrp, conv_p = _in_proj(
            xp, w, 256, conv_prefix=conv_m, n_seq=bp)
        a_p, d_p, pool_p, ssm_p = _mixer(
            (up, qnp, knp, vp), gbcp, gbrp, None, pool_m, ssm_m, w,
            nseq=1, seq_len=SLAB, n_valid=SLAB, pos0=N_META, n_groups=bp, n_steps=seq // SLAB,
            row_block0=0, shared_state=True)
        xp = _out_ffn(xp, a_p, d_p, zp, gatesp, w, 256)
        xs = _out_ffn(xs, jnp.concatenate([a_s, a_m], axis=0), jnp.concatenate([d_s, d_m], axis=0),
                      zs, gatess, w, small_tm)
        for acc, val in zip(outs, (conv_p, ssm_p, pool_p)):
            acc.append(val)
    y_prompt = xp.reshape(bp, seq, D_MODEL)
    y_sample = xs[:n_sample].reshape(bs, dec_seq, D_MODEL)
    conv_p, ssm_p, pool_p = (jnp.stack(o) for o in outs)
    conv_s, pool_s, ssm_s = sample_states
    return (y_prompt, y_sample, conv_p, ssm_p, pool_p, conv_s, ssm_s, pool_s)
```

```python
import functools

import jax
import jax.numpy as jnp
from jax import lax
from jax.experimental import pallas as pl
from jax.experimental.pallas import tpu as pltpu

D_MODEL = 1024
N_META = 16
POOL_GROUPS = 4
POOL_GROUP_W = 128
POOL_WIDTH = 512
POOL_WINDOWS = (2, 4, 8, 16)
POOL_STATE = 15
HEAD_K = 128
HEAD_V = 128
N_HEADS = 8
KEY_W = 1024
VAL_W = 1024
QKV_W = 3072
CONV_W = 4
D_FF = 4096
PAST_LEN = 16384
EPS = 1e-6

LANES = 128
SUBLANES = 8
SLAB = 128
PROJ_CHUNK = 256
GB_W = 128
VMEM_LIMIT = 56 * 1024 * 1024

F32 = jnp.float32
BF16 = jnp.bfloat16
HI = lax.Precision.HIGHEST
NEG_BIG = -1e30
NEG_LOG2_E = -1.4426950408889634


def _dot(a, b):
    return jnp.dot(a.astype(BF16), b.astype(BF16), preferred_element_type=F32)


def _dot_nt(a, b):
    return lax.dot_general(a.astype(BF16), b.astype(BF16), (((1,), (1,)), ((), ())),
                           preferred_element_type=F32)


def _dot_tn(a, b):
    return lax.dot_general(a.astype(BF16), b.astype(BF16), (((0,), (0,)), ((), ())),
                           preferred_element_type=F32)


def _dot_hi(a, b):
    return jnp.dot(a, b, precision=HI, preferred_element_type=F32)


def _rms(x, g):
    return x * lax.rsqrt(jnp.mean(x * x, axis=-1, keepdims=True) + EPS) * g


def _sigmoid(x):
    return 1.0 / (1.0 + jnp.exp2(x * NEG_LOG2_E))


def _silu(x):
    return x * _sigmoid(x)


def _softplus(x):
    return jnp.maximum(x, 0.0) + jnp.log1p(jnp.exp(-jnp.abs(x)))


def _l2_normalize(x, scale=1.0):
    return x * (lax.rsqrt(jnp.sum(x * x, axis=-1, keepdims=True) + EPS) * scale)


def _causal_conv_silu(ext_ref, seq, row0, n_rows, lanes, convw_ref):
    acc = None
    for j in range(CONV_W):
        lo = row0 - (CONV_W - 1) + j
        rows = ext_ref[lo:lo + n_rows, lanes] if seq is None else ext_ref[seq, lo:lo + n_rows, lanes]
        term = rows * convw_ref[j:j + 1, lanes]
        acc = term if acc is None else acc + term
    return _silu(acc)


def _beta_decay_cols(ba, pcol_ref):
    lane = lax.broadcasted_iota(jnp.int32, ba.shape, 1)
    g_col = -jnp.exp(pcol_ref[0:1, :]) * _softplus(ba + pcol_ref[1:2, :])
    return jnp.where(lane < N_HEADS, _sigmoid(ba), jnp.where(lane < 2 * N_HEADS, g_col, 0.0))


def _beta_decay_rows(bat, prow_ref):
    row = lax.broadcasted_iota(jnp.int32, bat.shape, 0)
    g_row = -jnp.exp(prow_ref[:, 0:1]) * _softplus(bat + prow_ref[:, 1:2])
    return jnp.where(row < N_HEADS, _sigmoid(bat), g_row)


def _in_proj_kernel(x_ref, g_ref, wu_ref, wqkv_ref, wz_ref, wg_ref, wba_ref, wbat_ref,
                    pcol_ref, prow_ref,
                    u_ref, qkv_ref, z_ref, gates_ref, gbc_ref, gbr_ref):
    h = _rms(x_ref[...], g_ref[...]).astype(BF16)
    u_ref[...] = jnp.dot(h, wu_ref[...], preferred_element_type=F32)
    qkv_ref[...] = jnp.dot(h, wqkv_ref[...], preferred_element_type=F32)
    z_ref[...] = jnp.dot(h, wz_ref[...], preferred_element_type=F32)
    gates_ref[...] = jnp.dot(h, wg_ref[...], preferred_element_type=F32)
    gbc_ref[...] = _beta_decay_cols(jnp.dot(h, wba_ref[...], preferred_element_type=F32), pcol_ref)
    bat = lax.dot_general(wbat_ref[...], h, (((1,), (1,)), ((), ())), preferred_element_type=F32)
    gbr_ref[...] = _beta_decay_rows(bat, prow_ref)


def _in_proj_conv_kernel(tm, n_tiles,
                         x_ref, g_ref, wu_ref, wqkv_ref, wz_ref, wg_ref, wba_ref, wbat_ref,
                         pcol_ref, prow_ref, convw_ref, convp_ref,
                         u_ref, qn_ref, kn_ref, v_ref, z_ref, gates_ref, gbc_ref, gbr_ref, convo_ref,
                         ext):
    t = pl.program_id(1)
    base = SUBLANES
    if n_tiles > 1:
        @pl.when(t > 0)
        def _():
            ext[base - (CONV_W - 1):base, :] = ext[base + tm - (CONV_W - 1):base + tm, :]

    @pl.when(t == 0)
    def _():
        ext[base - (CONV_W - 1):base, :] = convp_ref[0]

    h = _rms(x_ref[...], g_ref[...]).astype(BF16)
    u_ref[...] = jnp.dot(h, wu_ref[...], preferred_element_type=F32)
    ext[base:base + tm, :] = jnp.dot(h, wqkv_ref[...], preferred_element_type=F32)
    convo_ref[0] = ext[base + tm - (CONV_W - 1):base + tm, :]
    n_groups = QKV_W // LANES
    chunks = ([(z_ref, wz_ref, c0) for c0 in range(0, VAL_W, PROJ_CHUNK)]
              + [(gates_ref, wg_ref, c0) for c0 in range(0, 2 * D_MODEL, PROJ_CHUNK)])
    assert n_groups % len(chunks) == 0
    every = n_groups // len(chunks)
    for c in range(n_groups):
        lanes = slice(c * LANES, (c + 1) * LANES)
        head_lanes = slice((c % N_HEADS) * LANES, (c % N_HEADS + 1) * LANES)
        y = _causal_conv_silu(ext, None, base, tm, lanes, convw_ref)
        if c < N_HEADS:
            qn_ref[:, head_lanes] = _l2_normalize(y, HEAD_K ** -0.5)
        elif c < 2 * N_HEADS:
            kn_ref[:, head_lanes] = _l2_normalize(y)
        else:
            v_ref[:, head_lanes] = y
        if c % every == 0:
            out_ref, w_ref, c0 = chunks[c // every]
            out_ref[:, c0:c0 + PROJ_CHUNK] = jnp.dot(h, w_ref[:, c0:c0 + PROJ_CHUNK],
                                                     preferred_element_type=F32)
    gbc_ref[...] = _beta_decay_cols(jnp.dot(h, wba_ref[...], preferred_element_type=F32), pcol_ref)
    bat = lax.dot_general(wbat_ref[...], h, (((1,), (1,)), ((), ())), preferred_element_type=F32)
    gbr_ref[...] = _beta_decay_rows(bat, prow_ref)


def _in_proj(x, w, tm, conv_prefix=None, n_seq=1):
    n = x.shape[0]
    fused = conv_prefix is not None
    assert n % (tm * n_seq) == 0
    n_tiles = n // (tm * n_seq)
    if fused:
        grid = (n_seq, n_tiles)
        const = lambda s, t: (0, 0)
        tok = lambda s, t: (s * n_tiles + t, 0)
        tok_t = lambda s, t: (0, s * n_tiles + t)
    else:
        grid = (n_tiles,)
        const = lambda i: (0, 0)
        tok = lambda i: (i, 0)
        tok_t = lambda i: (0, i)
    resident = pl.Buffered(1)
    in_specs = [
        pl.BlockSpec((tm, D_MODEL), tok),
        pl.BlockSpec((1, D_MODEL), const),
        pl.BlockSpec((D_MODEL, POOL_WIDTH), const, pipeline_mode=resident),
        pl.BlockSpec((D_MODEL, QKV_W), const, pipeline_mode=resident),
        pl.BlockSpec((D_MODEL, VAL_W), const, pipeline_mode=resident),
        pl.BlockSpec((D_MODEL, 2 * D_MODEL), const, pipeline_mode=resident),
        pl.BlockSpec((D_MODEL, GB_W), const, pipeline_mode=resident),
        pl.BlockSpec((2 * N_HEADS, D_MODEL), const, pipeline_mode=resident),
        pl.BlockSpec((SUBLANES, GB_W), const),
        pl.BlockSpec((2 * N_HEADS, LANES), const),
    ]
    args = [x, w["g_pre_mix"], w["wu"], w["wqkv"], w["wz"], w["wg"], w["wba"], w["wbat"],
            w["pcol"], w["prow"]]
    tail_specs = [
        pl.BlockSpec((tm, 2 * D_MODEL), tok),
        pl.BlockSpec((tm, GB_W), tok),
        pl.BlockSpec((2 * N_HEADS, tm), tok_t),
    ]
    tail_shapes = [
        jax.ShapeDtypeStruct((n, 2 * D_MODEL), F32),
        jax.ShapeDtypeStruct((n, GB_W), F32),
        jax.ShapeDtypeStruct((2 * N_HEADS, n), F32),
    ]
    if fused:
        in_specs += [pl.BlockSpec((CONV_W, QKV_W), const),
                     pl.BlockSpec((1, CONV_W - 1, QKV_W), lambda s, t: (0, 0, 0))]
        args += [w["conv_w"], conv_prefix]
        out_specs = [
            pl.BlockSpec((tm, POOL_WIDTH), tok),
            pl.BlockSpec((tm, KEY_W), tok),
            pl.BlockSpec((tm, KEY_W), tok),
            pl.BlockSpec((tm, VAL_W), tok),
            pl.BlockSpec((tm, VAL_W), tok),
        ] + tail_specs + [pl.BlockSpec((1, CONV_W - 1, QKV_W), lambda s, t: (s, 0, 0))]
        out_shape = [
            jax.ShapeDtypeStruct((n, POOL_WIDTH), F32),
            jax.ShapeDtypeStruct((n, KEY_W), F32),
            jax.ShapeDtypeStruct((n, KEY_W), F32),
            jax.ShapeDtypeStruct((n, VAL_W), F32),
            jax.ShapeDtypeStruct((n, VAL_W), F32),
        ] + tail_shapes + [jax.ShapeDtypeStruct((n_seq, CONV_W - 1, QKV_W), F32)]
        body = functools.partial(_in_proj_conv_kernel, tm, n_tiles)
        scratch = [pltpu.VMEM((SUBLANES + tm, QKV_W), F32)]
        semantics = ("parallel", "arbitrary")
    else:
        out_specs = [
            pl.BlockSpec((tm, POOL_WIDTH), tok),
            pl.BlockSpec((tm, QKV_W), tok),
            pl.BlockSpec((tm, VAL_W), tok),
        ] + tail_specs
        out_shape = [
            jax.ShapeDtypeStruct((n, POOL_WIDTH), F32),
            jax.ShapeDtypeStruct((n, QKV_W), F32),
            jax.ShapeDtypeStruct((n, VAL_W), F32),
        ] + tail_shapes
        body = _in_proj_kernel
        scratch = []
        semantics = ("parallel",)
    return pl.pallas_call(
        body,
        grid=grid,
        in_specs=in_specs,
        out_specs=out_specs,
        out_shape=out_shape,
        scratch_shapes=scratch,
        compiler_params=pltpu.CompilerParams(
            dimension_semantics=semantics, vmem_limit_bytes=VMEM_LIMIT),
        name="in_proj_conv" if fused else "in_proj",
    )(*args)


def _mixer_kernel(nb, nseq, seq_len, n_valid, pos0, n_steps, preconv, shared_state, n_passthrough,
                  *refs):
    refs = list(refs)
    rest = refs[11 + n_passthrough:]
    if preconv:
        (u_ref, qn_ref, kn_ref, v_ref, gbc_ref, gbr_ref, poolp_ref, s0_ref,
         onorm_ref, poolw_ref, pscale_ref) = refs[:11]
        a_ref, d_ref, poolo_ref, so_ref, u_ext = rest
    else:
        (u_ref, qkv_ref, gbc_ref, gbr_ref, convp_ref, poolp_ref, s0_ref,
         convw_ref, onorm_ref, poolw_ref, pscale_ref) = refs[:11]
        a_ref, d_ref, convo_ref, poolo_ref, so_ref, qkv_ext, u_ext = rest
    t = pl.program_id(1)
    conv_base = SUBLANES
    pool_base = 2 * SUBLANES
    last = n_valid if n_valid < seq_len else seq_len
    seqs = [(b, s) for b in range(nb) for s in range(nseq)]
    sid = lambda b, s: b * nseq + s
    src = lambda b, s: 0 if shared_state else sid(b, s)

    if n_steps > 1:
        @pl.when(t > 0)
        def _():
            for b, s in seqs:
                i = sid(b, s)
                u_ext[i, pool_base - POOL_STATE:pool_base, :] = (
                    u_ext[i, pool_base + seq_len - POOL_STATE:pool_base + seq_len, :])
                if not preconv:
                    qkv_ext[i, conv_base - (CONV_W - 1):conv_base, :] = (
                        qkv_ext[i, conv_base + seq_len - (CONV_W - 1):conv_base + seq_len, :])

    @pl.when(t == 0)
    def _():
        for b, s in seqs:
            i = sid(b, s)
            u_ext[i, pool_base - POOL_STATE:pool_base, :] = poolp_ref[src(b, s)]
            if not preconv:
                qkv_ext[i, conv_base - (CONV_W - 1):conv_base, :] = convp_ref[src(b, s)]
            if n_steps > 1:
                so_ref[i] = s0_ref[src(b, s)]

    def state_in(b, s, h):
        return so_ref[sid(b, s), h] if n_steps > 1 else s0_ref[src(b, s), h]

    for b, s in seqs:
        i = sid(b, s)
        rows = slice(s * seq_len, (s + 1) * seq_len)
        u_ext[i, pool_base:pool_base + seq_len, :] = u_ref[b, rows, :]
        poolo_ref[i] = u_ext[i, pool_base + last - POOL_STATE:pool_base + last, :]
        if not preconv:
            qkv_ext[i, conv_base:conv_base + seq_len, :] = qkv_ref[b, rows, :]
            convo_ref[i] = qkv_ext[i, conv_base + last - (CONV_W - 1):conv_base + last, :]

    for b in range(nb):
        for gi, win in enumerate(POOL_WINDOWS):
            lanes = slice(gi * POOL_GROUP_W, (gi + 1) * POOL_GROUP_W)
            if pos0 >= POOL_STATE:
                cnt = float(win)
            else:
                pos = lax.broadcasted_iota(jnp.int32, (seq_len, 1), 0) + (pos0 + 1) + t * seq_len
                cnt = jnp.minimum(pos, win).astype(F32)
            pieces = []
            for s in range(nseq):
                i = sid(b, s)
                tok = u_ext[i, pool_base:pool_base + seq_len, lanes]
                tot = tok
                for j in range(1, win):
                    tot = tot + u_ext[i, pool_base - j:pool_base - j + seq_len, lanes]
                pieces.append(tot / cnt - tok)
            dev = pieces[0] if nseq == 1 else jnp.concatenate(pieces, axis=0)
            a_ref[b, :, lanes] = _dot(dev, poolw_ref[gi]) * pscale_ref[:, lanes]

    ri = lax.broadcasted_iota(jnp.int32, (SLAB, SLAB), 0)
    ci = lax.broadcasted_iota(jnp.int32, (SLAB, SLAB), 1)
    shift = seq_len.bit_length() - 1
    same = (ri >> shift) == (ci >> shift)
    m_le = same & (ci <= ri)
    m_lt = same & (ci < ri)
    gbc, gc_col, gc_row, gl_col = [], [], [], []
    for b in range(nb):
        gbc_b = gbc_ref[b]
        gbr_b = gbr_ref[b, 0]
        if n_valid < seq_len:
            gbc_b = jnp.where(lax.broadcasted_iota(jnp.int32, gbc_b.shape, 0) < n_valid, gbc_b, 0.0)
            gbr_b = jnp.where(lax.broadcasted_iota(jnp.int32, gbr_b.shape, 1) < n_valid, gbr_b, 0.0)
        gbc.append(gbc_b)
        gc_col.append(_dot_hi(m_le.astype(F32), gbc_b))
        gc_row.append(_dot_hi(gbr_b, (same & (ri <= ci)).astype(F32)))
        gl_col.append(_dot_hi(same.astype(F32), gbc_b))

    units = [(b, h) for b in range(nb) for h in range(N_HEADS)]
    heads = range(len(units))
    eye = (ri == ci).astype(F32)
    narrow = (lambda a: a.astype(BF16)) if seq_len % (2 * SUBLANES) == 0 else (lambda a: a)

    def conv(b, lanes):
        pieces = [_causal_conv_silu(qkv_ext, sid(b, s), conv_base, seq_len, lanes, convw_ref)
                  for s in range(nseq)]
        return pieces[0] if nseq == 1 else jnp.concatenate(pieces, axis=0)

    kq, k_op, v, beta, e_g, e_dec, carry, decay_le, decay_lb = [], [], [], [], [], [], [], [], []
    for b, h in units:
        hl = slice(h * HEAD_K, (h + 1) * HEAD_K)
        if preconv:
            q, k = narrow(qn_ref[b, :, hl]), narrow(kn_ref[b, :, hl])
            v.append(v_ref[b, :, hl])
        else:
            q = narrow(_l2_normalize(conv(b, hl), HEAD_K ** -0.5))
            k = narrow(_l2_normalize(conv(b, slice(KEY_W + h * HEAD_K, KEY_W + (h + 1) * HEAD_K))))
            v.append(conv(b, slice(2 * KEY_W + h * HEAD_V, 2 * KEY_W + (h + 1) * HEAD_V)))
        k_op.append(k)
        kq.append(jnp.concatenate([k, q], axis=0))
        bt = gbc[b][:, h:h + 1]
        g_c = gc_col[b][:, N_HEADS + h:N_HEADS + h + 1]
        g_r = gc_row[b][N_HEADS + h:N_HEADS + h + 1, :]
        g_l = gl_col[b][:, N_HEADS + h:N_HEADS + h + 1]
        d_le = jnp.exp(jnp.where(m_le, g_c - g_r, NEG_BIG))
        decay_le.append(d_le)
        decay_lb.append(jnp.where(m_lt, d_le, 0.0) * bt)
        beta.append(bt)
        e_g.append(jnp.exp(g_c))
        e_dec.append(jnp.exp(g_l - g_c))
        carry.append([jnp.exp(g_l[s * seq_len:s * seq_len + 1, :]) for s in range(nseq)])

    kk_qk = [_dot_nt(kq[h], k_op[h]) for h in heads]
    lmat = [kk_qk[h][:SLAB] * decay_lb[h] for h in heads]
    attn = [(kk_qk[h][SLAB:] * decay_le[h]).astype(BF16) for h in heads]

    k_s, q_s = [], []
    for u, (b, h) in enumerate(units):
        if nseq == 1:
            prod = _dot(kq[u], state_in(b, 0, h))
            k_s.append(prod[:SLAB])
            q_s.append(prod[SLAB:])
        else:
            ks_parts, qs_parts = [], []
            for s in range(nseq):
                rows = slice(s * seq_len, (s + 1) * seq_len)
                pair_rows = jnp.concatenate(
                    [kq[u][rows], kq[u][SLAB + s * seq_len:SLAB + (s + 1) * seq_len]], axis=0)
                prod = _dot(pair_rows, state_in(b, s, h))
                ks_parts.append(prod[:seq_len])
                qs_parts.append(prod[seq_len:])
            k_s.append(jnp.concatenate(ks_parts, axis=0))
            q_s.append(jnp.concatenate(qs_parts, axis=0))
    rhs = [(beta[h] * (v[h] - e_g[h] * k_s[h])).astype(BF16) for h in heads]

    pair = (ri >> 1) == (ci >> 1)
    l_bf = [lmat[h].astype(BF16) for h in heads]
    inv = [eye - jnp.where(pair, lmat[h], 0.0) for h in heads]
    blk = 2
    while blk < seq_len:
        sh = blk.bit_length() - 1
        lower_left = ((ri >> (sh + 1)) == (ci >> (sh + 1))) & ((ri >> sh) != (ci >> sh))
        inv_bf = [inv[h].astype(BF16) for h in heads]
        half = [_dot(inv_bf[h], l_bf[h]).astype(BF16) for h in heads]
        inv = [inv[h] - jnp.where(lower_left, _dot(half[h], inv_bf[h]), 0.0) for h in heads]
        blk *= 2

    v_new = [_dot(inv[h], rhs[h]) for h in heads]
    v_att = [narrow(v_new[h]) for h in heads]
    v_dec = [narrow(v_new[h] * e_dec[h]) for h in heads]
    for u, (b, h) in enumerate(units):
        for s in range(nseq):
            rows = slice(s * seq_len, (s + 1) * seq_len)
            so_ref[sid(b, s), h] = (state_in(b, s, h) * carry[u][s]
                                    + _dot_tn(k_op[u][rows], v_dec[u][rows]))
    for u, (b, h) in enumerate(units):
        o = e_g[u] * q_s[u] + _dot(attn[u], v_att[u])
        d_ref[b, :, h * HEAD_V:(h + 1) * HEAD_V] = (
            o * lax.rsqrt(jnp.mean(o * o, axis=-1, keepdims=True) + EPS) * onorm_ref[...])


def _mixer(tokens, gbc, gbr, convp, poolp, s0, w, *, nb, nseq, seq_len, n_valid, pos0,
           n_groups, n_steps, row0, shared_state, layer=None, prev_ssm=None):
    assert nseq * seq_len == SLAB and (nseq == 1 or n_steps == 1)
    assert layer is None or not shared_state
    preconv = convp is None
    n_tok = gbc.shape[0]
    row_len = n_steps * SLAB
    n_rows_in = n_tok // row_len
    n_rows_out = n_groups * nb
    seq_blk = nb * nseq
    n_seq_total = n_groups * seq_blk
    tok = lambda gi, t: (row0 + gi, t, 0)
    c2 = lambda gi, t: (0, 0)
    c3 = lambda gi, t: (0, 0, 0)

    def state_specs(per_seq, stacked_like=None):
        zeros = (0,) * len(per_seq)
        blk = (seq_blk,) + per_seq
        if stacked_like is not None:
            spec = pl.BlockSpec((None,) + blk, lambda gi, t: (layer, gi) + zeros)
            return spec, spec, jax.ShapeDtypeStruct(stacked_like.shape, F32)
        in_spec = (pl.BlockSpec((1,) + per_seq, lambda gi, t: (0,) + zeros) if shared_state
                   else pl.BlockSpec(blk, lambda gi, t: (gi,) + zeros))
        out_spec = pl.BlockSpec(blk, lambda gi, t: (gi,) + zeros)
        return in_spec, out_spec, jax.ShapeDtypeStruct((n_seq_total,) + per_seq, F32)

    pool_in, pool_out, pool_shape = state_specs((POOL_STATE, POOL_WIDTH))
    s0_spec, so_spec, so_shape = state_specs((N_HEADS, HEAD_K, HEAD_V),
                                             s0 if layer is not None else None)
    widths = [a.shape[1] for a in tokens]
    in_specs = [pl.BlockSpec((nb, SLAB, wd), tok) for wd in widths]
    in_specs += [pl.BlockSpec((nb, SLAB, GB_W), tok),
                 pl.BlockSpec((nb, 1, 2 * N_HEADS, SLAB), lambda gi, t: (row0 + gi, t, 0, 0))]
    gbr_slabs = gbr.reshape(2 * N_HEADS, n_rows_in, n_steps, SLAB).transpose(1, 2, 0, 3)
    args = [a.reshape(n_rows_in, row_len, a.shape[1]) for a in tokens]
    args += [gbc.reshape(n_rows_in, row_len, GB_W), gbr_slabs]
    if not preconv:
        conv_in, conv_out, conv_shape = state_specs((CONV_W - 1, QKV_W))
        in_specs.append(conv_in)
        args.append(convp)
    in_specs += [pool_in, s0_spec]
    args += [poolp, s0]
    if not preconv:
        in_specs.append(pl.BlockSpec((CONV_W, QKV_W), c2))
        args.append(w["conv_w"])
    in_specs += [pl.BlockSpec((1, HEAD_V), c2),
                 pl.BlockSpec((POOL_GROUPS, POOL_GROUP_W, POOL_GROUP_W), c3),
                 pl.BlockSpec((1, POOL_WIDTH), c2)]
    args += [w["o_norm_g"], w["pool_w"], w["pool_scale"]]
    out_tok = lambda gi, t: (gi, t, 0)
    out_specs = [pl.BlockSpec((nb, SLAB, POOL_WIDTH), out_tok),
                 pl.BlockSpec((nb, SLAB, VAL_W), out_tok)]
    out_shape = [jax.ShapeDtypeStruct((n_rows_out, row_len, POOL_WIDTH), F32),
                 jax.ShapeDtypeStruct((n_rows_out, row_len, VAL_W), F32)]
    scratch = []
    if not preconv:
        out_specs.append(conv_out)
        out_shape.append(conv_shape)
        scratch.append(pltpu.VMEM((seq_blk, SUBLANES + seq_len, QKV_W), F32))
    out_specs += [pool_out, so_spec]
    out_shape += [pool_shape, so_shape]
    scratch.append(pltpu.VMEM((seq_blk, 2 * SUBLANES + seq_len, POOL_WIDTH), F32))
    aliases = {}
    if prev_ssm is not None:
        in_specs.append(pl.BlockSpec(memory_space=pl.ANY))
        args.append(prev_ssm)
        aliases[len(args) - 1] = len(out_shape) - 1
    outs = pl.pallas_call(
        functools.partial(_mixer_kernel, nb, nseq, seq_len, n_valid, pos0, n_steps, preconv,
                          shared_state, len(aliases)),
        grid=(n_groups, n_steps),
        in_specs=in_specs,
        out_specs=out_specs,
        out_shape=out_shape,
        scratch_shapes=scratch,
        input_output_aliases=aliases,
        compiler_params=pltpu.CompilerParams(
            dimension_semantics=("parallel", "arbitrary"), vmem_limit_bytes=VMEM_LIMIT),
        name="mixer_%dx%dx%d%s" % (nb, nseq, seq_len, "_preconv" if preconv else ""),
    )(*args)
    a_out, d_out = (o.reshape(n_rows_out * row_len, o.shape[2]) for o in outs[:2])
    return [a_out, d_out] + list(outs[2:])


def _out_ffn_kernel(x_ref, a_ref, d_ref, z_ref, gates_ref, wbp_ref, wbd_ref, wout_ref, wup_ref,
                    wdown_ref, gpm_ref, gpf_ref, gqf_ref, y_ref):
    br_pool = jnp.dot(a_ref[...].astype(BF16), wbp_ref[...], preferred_element_type=F32)
    delta = (d_ref[...] * _silu(z_ref[...])).astype(BF16)
    br_delta = jnp.dot(delta, wbd_ref[...], preferred_element_type=F32)
    gate_pool = _sigmoid(gates_ref[:, :D_MODEL])
    gate_delta = _sigmoid(gates_ref[:, D_MODEL:])
    merged = (gate_pool * br_pool + gate_delta * br_delta).astype(BF16)
    m = jnp.dot(merged, wout_ref[...], preferred_element_type=F32)
    x = x_ref[...] + _rms(m, gpm_ref[...])
    h2 = _rms(x, gpf_ref[...]).astype(BF16)
    up = jnp.dot(h2, wup_ref[...], preferred_element_type=F32)
    act = jnp.square(jnp.maximum(up, 0.0)).astype(BF16)
    f = jnp.dot(act, wdown_ref[...], preferred_element_type=F32)
    y_ref[...] = x + _rms(f, gqf_ref[...])


def _out_ffn(x, a, d, z, gates, w, tm, layer):
    n = x.shape[0]
    assert n % tm == 0
    const = lambda i: (0, 0)
    tok = lambda i: (i, 0)
    resident = pl.Buffered(1)

    def weight(rows, cols):
        return pl.BlockSpec((None, rows, cols), lambda i: (layer, 0, 0), pipeline_mode=resident)

    in_specs = [
        pl.BlockSpec((tm, D_MODEL), tok),
        pl.BlockSpec((tm, POOL_WIDTH), tok),
        pl.BlockSpec((tm, VAL_W), tok),
        pl.BlockSpec((tm, VAL_W), tok),
        pl.BlockSpec((tm, 2 * D_MODEL), tok),
        weight(POOL_WIDTH, D_MODEL),
        weight(VAL_W, D_MODEL),
        weight(D_MODEL, D_MODEL),
        weight(D_MODEL, D_FF),
        weight(D_FF, D_MODEL),
        pl.BlockSpec((1, D_MODEL), const),
        pl.BlockSpec((1, D_MODEL), const),
        pl.BlockSpec((1, D_MODEL), const),
    ]
    return pl.pallas_call(
        _out_ffn_kernel,
        grid=(n // tm,),
        in_specs=in_specs,
        out_specs=pl.BlockSpec((tm, D_MODEL), tok),
        out_shape=jax.ShapeDtypeStruct((n, D_MODEL), F32),
        compiler_params=pltpu.CompilerParams(
            dimension_semantics=("parallel",), vmem_limit_bytes=VMEM_LIMIT),
        name="out_ffn",
    )(x, a, d, z, gates, w["wbp"], w["wbd"], w["wout"], w["wup"], w["wdown"],
      w["g_post_mix"], w["g_pre_ffn"], w["g_post_ffn"])


def _layer_weights(l, g_pre_mix, w_in, conv_w, a_log, dt_bias, o_norm_g, pool_w, pool_scale,
                   w_branch_pool, w_branch_delta, w_out, g_post_mix, g_pre_ffn, w_up, w_down,
                   g_post_ffn):
    wi = w_in[l]
    o_qkv = POOL_WIDTH
    o_b = o_qkv + QKV_W
    o_z = o_b + 2 * N_HEADS
    o_g = o_z + VAL_W
    wba = lax.optimization_barrier(wi[:, o_b:o_z])
    zeros8 = jnp.zeros((N_HEADS,), F32)
    a_pad = jnp.concatenate([zeros8, a_log[l]])
    dt_pad = jnp.concatenate([zeros8, dt_bias[l]])
    pcol = jnp.zeros((SUBLANES, GB_W), F32).at[0, :2 * N_HEADS].set(a_pad).at[1, :2 * N_HEADS].set(dt_pad)
    prow = jnp.zeros((2 * N_HEADS, LANES), F32).at[:, 0].set(a_pad).at[:, 1].set(dt_pad)
    return {
        "g_pre_mix": g_pre_mix[l][None],
        "wu": wi[:, :o_qkv].astype(BF16),
        "wqkv": wi[:, o_qkv:o_b].astype(BF16),
        "wz": wi[:, o_z:o_g].astype(BF16),
        "wg": wi[:, o_g:].astype(BF16),
        "wba": jnp.pad(wba, ((0, 0), (0, GB_W - 2 * N_HEADS))).astype(BF16),
        "wbat": wba.T.astype(BF16),
        "pcol": pcol,
        "prow": prow,
        "conv_w": conv_w[l],
        "o_norm_g": o_norm_g[l][None],
        "pool_w": pool_w[l].astype(BF16),
        "pool_scale": pool_scale[l][None],
        "wbp": w_branch_pool.astype(BF16),
        "wbd": w_branch_delta.astype(BF16),
        "wout": w_out.astype(BF16),
        "wup": w_up.astype(BF16),
        "wdown": w_down.astype(BF16),
        "g_post_mix": g_post_mix[l][None],
        "g_pre_ffn": g_pre_ffn[l][None],
        "g_post_ffn": g_post_ffn[l][None],
    }


def kernel(x_prompt, x_sample, state_conv, state_ssm, state_pool, meta_tokens, g_pre_mix, w_in, conv_w, a_log, dt_bias, o_norm_g, pool_w, pool_scale, w_branch_pool, w_branch_delta, w_out, g_post_mix, g_pre_ffn, w_up, w_down, g_post_ffn):
    bp, seq, _ = x_prompt.shape
    bs, dec_seq, _ = x_sample.shape
    depth = w_in.shape[0]
    n_prompt = bp * seq
    n_sample = bs * dec_seq
    assert seq % 256 == 0 and n_sample % SLAB == 0 and SLAB % dec_seq == 0 and N_META <= SLAB
    xp = x_prompt.reshape(n_prompt, D_MODEL)
    xs = jnp.concatenate([x_sample.reshape(n_sample, D_MODEL), meta_tokens.astype(F32),
                          jnp.zeros((SLAB - N_META, D_MODEL), F32)], axis=0)
    n_small = n_sample + SLAB
    small_tm = 384 if n_small % 384 == 0 else SLAB
    meta_block = n_sample // SLAB
    zero_conv = jnp.zeros((1, CONV_W - 1, QKV_W), F32)
    zero_pool = jnp.zeros((1, POOL_STATE, POOL_WIDTH), F32)
    zero_ssm = jnp.zeros((1, N_HEADS, HEAD_K, HEAD_V), F32)
    state_ssm = state_ssm.astype(F32)
    prompt_nb = 2 if bp % 2 == 0 else 1
    outs = [[] for _ in range(5)]
    ssm_s = None
    for l in range(depth):
        w = _layer_weights(l, g_pre_mix, w_in, conv_w, a_log, dt_bias, o_norm_g, pool_w, pool_scale,
                           w_branch_pool, w_branch_delta, w_out, g_post_mix, g_pre_ffn, w_up,
                           w_down, g_post_ffn)
        us, qkvs, zs, gatess, gbcs, gbrs = _in_proj(xs, w, small_tm)
        a_m, d_m, conv_m, pool_m, ssm_m = _mixer(
            (us, qkvs), gbcs, gbrs, zero_conv, zero_pool, zero_ssm, w,
            nb=1, nseq=1, seq_len=SLAB, n_valid=N_META, pos0=0, n_groups=1, n_steps=1,
            row0=meta_block, shared_state=True)
        a_s, d_s, conv_s, pool_s, ssm_s = _mixer(
            (us, qkvs), gbcs, gbrs, state_conv[l], state_pool[l], state_ssm, w,
            nb=1, nseq=SLAB // dec_seq, seq_len=dec_seq, n_valid=dec_seq, pos0=PAST_LEN,
            n_groups=n_sample // SLAB, n_steps=1, row0=0, shared_state=False,
            layer=l, prev_ssm=ssm_s)
        up, qnp, knp, vp, zp, gatesp, gbcp, gbrp, conv_p = _in_proj(
            xp, w, 256, conv_prefix=conv_m, n_seq=bp)
        a_p, d_p, pool_p, ssm_p = _mixer(
            (up, qnp, knp, vp), gbcp, gbrp, None, pool_m, ssm_m, w,
            nb=prompt_nb, nseq=1, seq_len=SLAB, n_valid=SLAB, pos0=N_META,
            n_groups=bp // prompt_nb, n_steps=seq // SLAB, row0=0, shared_state=True)
        xp = _out_ffn(xp, a_p, d_p, zp, gatesp, w, 256, l)
        xs = _out_ffn(xs, jnp.concatenate([a_s, a_m], axis=0), jnp.concatenate([d_s, d_m], axis=0),
                      zs, gatess, w, small_tm, l)
        for acc, val in zip(outs, (conv_p, ssm_p, pool_p, conv_s, pool_s)):
            acc.append(val)
    y_prompt = xp.reshape(bp, seq, D_MODEL)
    y_sample = xs[:n_sample].reshape(bs, dec_seq, D_MODEL)
    conv_p, ssm_p, pool_p, conv_s, pool_s = (jnp.stack(o) for o in outs)
    return (y_prompt, y_sample, conv_p, ssm_p, pool_p, conv_s, ssm_s, pool_s)
```

```python
import functools

import jax
import jax.numpy as jnp
from jax import lax
from jax.experimental import pallas as pl
from jax.experimental.pallas import tpu as pltpu

D_MODEL = 1024
N_META = 16
POOL_GROUPS = 4
POOL_GROUP_W = 128
POOL_WIDTH = 512
POOL_WINDOWS = (2, 4, 8, 16)
POOL_STATE = 15
HEAD_K = 128
HEAD_V = 128
N_HEADS = 8
KEY_W = 1024
VAL_W = 1024
QKV_W = 3072
CONV_W = 4
D_FF = 4096
PAST_LEN = 16384
EPS = 1e-6

LANES = 128
SUBLANES = 8
SLAB = 128
PROJ_CHUNK = 256
GB_W = 128
VMEM_LIMIT = 56 * 1024 * 1024

F32 = jnp.float32
BF16 = jnp.bfloat16
HI = lax.Precision.HIGHEST
NEG_BIG = -1e30
NEG_LOG2_E = -1.4426950408889634


def _dot(a, b):
    return jnp.dot(a.astype(BF16), b.astype(BF16), preferred_element_type=F32)


def _dot_nt(a, b):
    return lax.dot_general(a.astype(BF16), b.astype(BF16), (((1,), (1,)), ((), ())),
                           preferred_element_type=F32)


def _dot_tn(a, b):
    return lax.dot_general(a.astype(BF16), b.astype(BF16), (((0,), (0,)), ((), ())),
                           preferred_element_type=F32)


def _dot_hi(a, b):
    return jnp.dot(a, b, precision=HI, preferred_element_type=F32)


def _rms(x, g):
    return x * lax.rsqrt(jnp.mean(x * x, axis=-1, keepdims=True) + EPS) * g


def _sigmoid(x):
    return 1.0 / (1.0 + jnp.exp2(x * NEG_LOG2_E))


def _silu(x):
    return x * _sigmoid(x)


def _softplus(x):
    return jnp.maximum(x, 0.0) + jnp.log1p(jnp.exp(-jnp.abs(x)))


def _l2_normalize(x, scale=1.0):
    return x * (lax.rsqrt(jnp.sum(x * x, axis=-1, keepdims=True) + EPS) * scale)


def _causal_conv_silu(ext_ref, seq, row0, n_rows, lanes, convw_ref):
    if seq is None and row0 >= SUBLANES and row0 % SUBLANES == 0 and n_rows % SUBLANES == 0:
        full = ext_ref[row0 - SUBLANES:row0 + n_rows, lanes]
        acc = full[SUBLANES:] * convw_ref[CONV_W - 1:CONV_W, lanes]
        for d in range(1, CONV_W):
            tap = pltpu.roll(full, d, axis=0)[SUBLANES:]
            acc = acc + tap * convw_ref[CONV_W - 1 - d:CONV_W - d, lanes]
        return _silu(acc)
    acc = None
    for j in range(CONV_W):
        lo = row0 - (CONV_W - 1) + j
        rows = ext_ref[lo:lo + n_rows, lanes] if seq is None else ext_ref[seq, lo:lo + n_rows, lanes]
        term = rows * convw_ref[j:j + 1, lanes]
        acc = term if acc is None else acc + term
    return _silu(acc)


def _beta_decay_cols(ba, pcol_ref):
    lane = lax.broadcasted_iota(jnp.int32, ba.shape, 1)
    g_col = -jnp.exp(pcol_ref[0:1, :]) * _softplus(ba + pcol_ref[1:2, :])
    return jnp.where(lane < N_HEADS, _sigmoid(ba), jnp.where(lane < 2 * N_HEADS, g_col, 0.0))


def _beta_decay_rows(bat, prow_ref):
    row = lax.broadcasted_iota(jnp.int32, bat.shape, 0)
    g_row = -jnp.exp(prow_ref[:, 0:1]) * _softplus(bat + prow_ref[:, 1:2])
    return jnp.where(row < N_HEADS, _sigmoid(bat), g_row)


def _in_proj_kernel(x_ref, g_ref, wu_ref, wqkv_ref, wz_ref, wg_ref, wba_ref, wbat_ref,
                    pcol_ref, prow_ref,
                    u_ref, qkv_ref, z_ref, gates_ref, gbc_ref, gbr_ref):
    h = _rms(x_ref[...], g_ref[...]).astype(BF16)
    u_ref[...] = jnp.dot(h, wu_ref[...], preferred_element_type=F32)
    qkv_ref[...] = jnp.dot(h, wqkv_ref[...], preferred_element_type=F32)
    z_ref[...] = jnp.dot(h, wz_ref[...], preferred_element_type=F32)
    gates_ref[...] = jnp.dot(h, wg_ref[...], preferred_element_type=F32)
    gbc_ref[...] = _beta_decay_cols(jnp.dot(h, wba_ref[...], preferred_element_type=F32), pcol_ref)
    bat = lax.dot_general(wbat_ref[...], h, (((1,), (1,)), ((), ())), preferred_element_type=F32)
    gbr_ref[...] = _beta_decay_rows(bat, prow_ref)


def _in_proj_conv_kernel(tm, n_tiles,
                         x_ref, g_ref, wu_ref, wqkv_ref, wz_ref, wg_ref, wba_ref, wbat_ref,
                         pcol_ref, prow_ref, convw_ref, convp_ref,
                         u_ref, qn_ref, kn_ref, v_ref, z_ref, gates_ref, gbc_ref, gbr_ref, convo_ref,
                         ext):
    t = pl.program_id(1)
    base = SUBLANES
    if n_tiles > 1:
        @pl.when(t > 0)
        def _():
            ext[base - (CONV_W - 1):base, :] = ext[base + tm - (CONV_W - 1):base + tm, :]

    @pl.when(t == 0)
    def _():
        ext[base - (CONV_W - 1):base, :] = convp_ref[0]

    h = _rms(x_ref[...], g_ref[...]).astype(BF16)
    u_ref[...] = jnp.dot(h, wu_ref[...], preferred_element_type=F32)
    n_groups = QKV_W // LANES
    chunks = ([(z_ref, wz_ref, c0) for c0 in range(0, VAL_W, PROJ_CHUNK)]
              + [(gates_ref, wg_ref, c0) for c0 in range(0, 2 * D_MODEL, PROJ_CHUNK)])
    assert n_groups % len(chunks) == 0
    every = n_groups // len(chunks)
    groups_per_chunk = PROJ_CHUNK // LANES
    for c in range(n_groups):
        if c % groups_per_chunk == 0:
            cols = slice(c * LANES, c * LANES + PROJ_CHUNK)
            ext[base:base + tm, cols] = jnp.dot(h, wqkv_ref[:, cols], preferred_element_type=F32)
            convo_ref[0, :, cols] = ext[base + tm - (CONV_W - 1):base + tm, cols]
        lanes = slice(c * LANES, (c + 1) * LANES)
        head_lanes = slice((c % N_HEADS) * LANES, (c % N_HEADS + 1) * LANES)
        y = _causal_conv_silu(ext, None, base, tm, lanes, convw_ref)
        if c < N_HEADS:
            qn_ref[:, head_lanes] = _l2_normalize(y, HEAD_K ** -0.5)
        elif c < 2 * N_HEADS:
            kn_ref[:, head_lanes] = _l2_normalize(y)
        else:
            v_ref[:, head_lanes] = y
        if c % every == 0:
            out_ref, w_ref, c0 = chunks[c // every]
            out_ref[:, c0:c0 + PROJ_CHUNK] = jnp.dot(h, w_ref[:, c0:c0 + PROJ_CHUNK],
                                                     preferred_element_type=F32)
    gbc_ref[...] = _beta_decay_cols(jnp.dot(h, wba_ref[...], preferred_element_type=F32), pcol_ref)
    bat = lax.dot_general(wbat_ref[...], h, (((1,), (1,)), ((), ())), preferred_element_type=F32)
    gbr_ref[...] = _beta_decay_rows(bat, prow_ref)


def _in_proj(x, w, tm, conv_prefix=None, n_seq=1):
    n = x.shape[0]
    fused = conv_prefix is not None
    assert n % (tm * n_seq) == 0
    n_tiles = n // (tm * n_seq)
    if fused:
        grid = (n_seq, n_tiles)
        const = lambda s, t: (0, 0)
        tok = lambda s, t: (s * n_tiles + t, 0)
        tok_t = lambda s, t: (0, s * n_tiles + t)
    else:
        grid = (n_tiles,)
        const = lambda i: (0, 0)
        tok = lambda i: (i, 0)
        tok_t = lambda i: (0, i)
    resident = pl.Buffered(1)
    in_specs = [
        pl.BlockSpec((tm, D_MODEL), tok),
        pl.BlockSpec((1, D_MODEL), const),
        pl.BlockSpec((D_MODEL, POOL_WIDTH), const, pipeline_mode=resident),
        pl.BlockSpec((D_MODEL, QKV_W), const, pipeline_mode=resident),
        pl.BlockSpec((D_MODEL, VAL_W), const, pipeline_mode=resident),
        pl.BlockSpec((D_MODEL, 2 * D_MODEL), const, pipeline_mode=resident),
        pl.BlockSpec((D_MODEL, GB_W), const, pipeline_mode=resident),
        pl.BlockSpec((2 * N_HEADS, D_MODEL), const, pipeline_mode=resident),
        pl.BlockSpec((SUBLANES, GB_W), const),
        pl.BlockSpec((2 * N_HEADS, LANES), const),
    ]
    args = [x, w["g_pre_mix"], w["wu"], w["wqkv"], w["wz"], w["wg"], w["wba"], w["wbat"],
            w["pcol"], w["prow"]]
    tail_specs = [
        pl.BlockSpec((tm, 2 * D_MODEL), tok),
        pl.BlockSpec((tm, GB_W), tok),
        pl.BlockSpec((2 * N_HEADS, tm), tok_t),
    ]
    tail_shapes = [
        jax.ShapeDtypeStruct((n, 2 * D_MODEL), F32),
        jax.ShapeDtypeStruct((n, GB_W), F32),
        jax.ShapeDtypeStruct((2 * N_HEADS, n), F32),
    ]
    if fused:
        in_specs += [pl.BlockSpec((CONV_W, QKV_W), const),
                     pl.BlockSpec((1, CONV_W - 1, QKV_W), lambda s, t: (0, 0, 0))]
        args += [w["conv_w"], conv_prefix]
        out_specs = [
            pl.BlockSpec((tm, POOL_WIDTH), tok),
            pl.BlockSpec((tm, KEY_W), tok),
            pl.BlockSpec((tm, KEY_W), tok),
            pl.BlockSpec((tm, VAL_W), tok),
            pl.BlockSpec((tm, VAL_W), tok),
        ] + tail_specs + [pl.BlockSpec((1, CONV_W - 1, QKV_W), lambda s, t: (s, 0, 0))]
        out_shape = [
            jax.ShapeDtypeStruct((n, POOL_WIDTH), F32),
            jax.ShapeDtypeStruct((n, KEY_W), F32),
            jax.ShapeDtypeStruct((n, KEY_W), F32),
            jax.ShapeDtypeStruct((n, VAL_W), F32),
            jax.ShapeDtypeStruct((n, VAL_W), F32),
        ] + tail_shapes + [jax.ShapeDtypeStruct((n_seq, CONV_W - 1, QKV_W), F32)]
        body = functools.partial(_in_proj_conv_kernel, tm, n_tiles)
        scratch = [pltpu.VMEM((SUBLANES + tm, QKV_W), F32)]
        semantics = ("parallel", "arbitrary")
    else:
        out_specs = [
            pl.BlockSpec((tm, POOL_WIDTH), tok),
            pl.BlockSpec((tm, QKV_W), tok),
            pl.BlockSpec((tm, VAL_W), tok),
        ] + tail_specs
        out_shape = [
            jax.ShapeDtypeStruct((n, POOL_WIDTH), F32),
            jax.ShapeDtypeStruct((n, QKV_W), F32),
            jax.ShapeDtypeStruct((n, VAL_W), F32),
        ] + tail_shapes
        body = _in_proj_kernel
        scratch = []
        semantics = ("parallel",)
    return pl.pallas_call(
        body,
        grid=grid,
        in_specs=in_specs,
        out_specs=out_specs,
        out_shape=out_shape,
        scratch_shapes=scratch,
        compiler_params=pltpu.CompilerParams(
            dimension_semantics=semantics, vmem_limit_bytes=VMEM_LIMIT),
        name="in_proj_conv" if fused else "in_proj",
    )(*args)


def _mixer_kernel(nb, nseq, seq_len, n_valid, pos0, n_steps, preconv, shared_state, n_passthrough,
                  *refs):
    refs = list(refs)
    rest = refs[11 + n_passthrough:]
    if preconv:
        (u_ref, qn_ref, kn_ref, v_ref, gbc_ref, gbr_ref, poolp_ref, s0_ref,
         onorm_ref, poolw_ref, pscale_ref) = refs[:11]
        a_ref, d_ref, poolo_ref, so_ref, u_ext = rest
    else:
        (u_ref, qkv_ref, gbc_ref, gbr_ref, convp_ref, poolp_ref, s0_ref,
         convw_ref, onorm_ref, poolw_ref, pscale_ref) = refs[:11]
        a_ref, d_ref, convo_ref, poolo_ref, so_ref, qkv_ext, u_ext = rest
    t = pl.program_id(1)
    conv_base = SUBLANES
    pool_base = 2 * SUBLANES
    last = n_valid if n_valid < seq_len else seq_len
    seqs = [(b, s) for b in range(nb) for s in range(nseq)]
    sid = lambda b, s: b * nseq + s
    src = lambda b, s: 0 if shared_state else sid(b, s)

    if n_steps > 1:
        @pl.when(t > 0)
        def _():
            for b, s in seqs:
                i = sid(b, s)
                u_ext[i, pool_base - POOL_STATE:pool_base, :] = (
                    u_ext[i, pool_base + seq_len - POOL_STATE:pool_base + seq_len, :])
                if not preconv:
                    qkv_ext[i, conv_base - (CONV_W - 1):conv_base, :] = (
                        qkv_ext[i, conv_base + seq_len - (CONV_W - 1):conv_base + seq_len, :])

    @pl.when(t == 0)
    def _():
        for b, s in seqs:
            i = sid(b, s)
            u_ext[i, pool_base - POOL_STATE:pool_base, :] = poolp_ref[src(b, s)]
            if not preconv:
                qkv_ext[i, conv_base - (CONV_W - 1):conv_base, :] = convp_ref[src(b, s)]
            if n_steps > 1:
                so_ref[i] = s0_ref[src(b, s)]

    def state_in(b, s, h):
        return so_ref[sid(b, s), h] if n_steps > 1 else s0_ref[src(b, s), h]

    for b, s in seqs:
        i = sid(b, s)
        rows = slice(s * seq_len, (s + 1) * seq_len)
        u_ext[i, pool_base:pool_base + seq_len, :] = u_ref[b, rows, :]
        poolo_ref[i] = u_ext[i, pool_base + last - POOL_STATE:pool_base + last, :]
        if not preconv:
            qkv_ext[i, conv_base:conv_base + seq_len, :] = qkv_ref[b, rows, :]
            convo_ref[i] = qkv_ext[i, conv_base + last - (CONV_W - 1):conv_base + last, :]

    for b in range(nb):
        for gi, win in enumerate(POOL_WINDOWS):
            lanes = slice(gi * POOL_GROUP_W, (gi + 1) * POOL_GROUP_W)
            if pos0 >= POOL_STATE:
                cnt = float(win)
            else:
                pos = lax.broadcasted_iota(jnp.int32, (seq_len, 1), 0) + (pos0 + 1) + t * seq_len
                cnt = jnp.minimum(pos, win).astype(F32)
            pieces = []
            for s in range(nseq):
                i = sid(b, s)
                tok = u_ext[i, pool_base:pool_base + seq_len, lanes]
                tot = tok
                for j in range(1, win):
                    tot = tot + u_ext[i, pool_base - j:pool_base - j + seq_len, lanes]
                pieces.append(tot / cnt - tok)
            dev = pieces[0] if nseq == 1 else jnp.concatenate(pieces, axis=0)
            a_ref[b, :, lanes] = _dot(dev, poolw_ref[gi]) * pscale_ref[:, lanes]

    ri = lax.broadcasted_iota(jnp.int32, (SLAB, SLAB), 0)
    ci = lax.broadcasted_iota(jnp.int32, (SLAB, SLAB), 1)
    shift = seq_len.bit_length() - 1
    same = (ri >> shift) == (ci >> shift)
    m_le = same & (ci <= ri)
    m_lt = same & (ci < ri)
    gbc, gc_col, gc_row, gl_col = [], [], [], []
    for b in range(nb):
        gbc_b = gbc_ref[b]
        gbr_b = gbr_ref[b, 0]
        if n_valid < seq_len:
            gbc_b = jnp.where(lax.broadcasted_iota(jnp.int32, gbc_b.shape, 0) < n_valid, gbc_b, 0.0)
            gbr_b = jnp.where(lax.broadcasted_iota(jnp.int32, gbr_b.shape, 1) < n_valid, gbr_b, 0.0)
        gbc.append(gbc_b)
        gc_col.append(_dot_hi(m_le.astype(F32), gbc_b))
        gc_row.append(_dot_hi(gbr_b, (same & (ri <= ci)).astype(F32)))
        gl_col.append(_dot_hi(same.astype(F32), gbc_b))

    units = [(b, h) for b in range(nb) for h in range(N_HEADS)]
    heads = range(len(units))
    eye = (ri == ci).astype(F32)
    narrow = (lambda a: a.astype(BF16)) if seq_len % (2 * SUBLANES) == 0 else (lambda a: a)

    def conv(b, lanes):
        pieces = [_causal_conv_silu(qkv_ext, sid(b, s), conv_base, seq_len, lanes, convw_ref)
                  for s in range(nseq)]
        return pieces[0] if nseq == 1 else jnp.concatenate(pieces, axis=0)

    kq, k_op, v, beta, e_g, e_dec, carry, decay_le, decay_lb = [], [], [], [], [], [], [], [], []
    for b, h in units:
        hl = slice(h * HEAD_K, (h + 1) * HEAD_K)
        if preconv:
            q, k = narrow(qn_ref[b, :, hl]), narrow(kn_ref[b, :, hl])
            v.append(v_ref[b, :, hl])
        else:
            q = narrow(_l2_normalize(conv(b, hl), HEAD_K ** -0.5))
            k = narrow(_l2_normalize(conv(b, slice(KEY_W + h * HEAD_K, KEY_W + (h + 1) * HEAD_K))))
            v.append(conv(b, slice(2 * KEY_W + h * HEAD_V, 2 * KEY_W + (h + 1) * HEAD_V)))
        k_op.append(k)
        kq.append(jnp.concatenate([k, q], axis=0))

    kk_qk = [_dot_nt(kq[h], k_op[h]) for h in heads]

    k_s, q_s = [], []
    for u, (b, h) in enumerate(units):
        if nseq == 1:
            prod = _dot(kq[u], state_in(b, 0, h))
            k_s.append(prod[:SLAB])
            q_s.append(prod[SLAB:])
        else:
            ks_parts, qs_parts = [], []
            for s in range(nseq):
                rows = slice(s * seq_len, (s + 1) * seq_len)
                pair_rows = jnp.concatenate(
                    [kq[u][rows], kq[u][SLAB + s * seq_len:SLAB + (s + 1) * seq_len]], axis=0)
                prod = _dot(pair_rows, state_in(b, s, h))
                ks_parts.append(prod[:seq_len])
                qs_parts.append(prod[seq_len:])
            k_s.append(jnp.concatenate(ks_parts, axis=0))
            q_s.append(jnp.concatenate(qs_parts, axis=0))

    for b, h in units:
        bt = gbc[b][:, h:h + 1]
        g_c = gc_col[b][:, N_HEADS + h:N_HEADS + h + 1]
        g_r = gc_row[b][N_HEADS + h:N_HEADS + h + 1, :]
        g_l = gl_col[b][:, N_HEADS + h:N_HEADS + h + 1]
        d_le = jnp.exp(jnp.where(m_le, g_c - g_r, NEG_BIG))
        decay_le.append(d_le)
        decay_lb.append(jnp.where(m_lt, d_le, 0.0) * bt)
        beta.append(bt)
        e_g.append(jnp.exp(g_c))
        e_dec.append(jnp.exp(g_l - g_c))
        carry.append([jnp.exp(g_l[s * seq_len:s * seq_len + 1, :]) for s in range(nseq)])

    lmat = [kk_qk[h][:SLAB] * decay_lb[h] for h in heads]
    attn = [(kk_qk[h][SLAB:] * decay_le[h]).astype(BF16) for h in heads]
    rhs =[(beta[h] * (v[h] - e_g[h] * k_s[h])).astype(BF16) for h in heads]

    pair = (ri >> 1) == (ci >> 1)
    l_bf = [lmat[h].astype(BF16) for h in heads]
    inv = [eye - jnp.where(pair, lmat[h], 0.0) for h in heads]
    blk = 2
    while blk < seq_len:
        sh = blk.bit_length() - 1
        lower_left = ((ri >> (sh + 1)) == (ci >> (sh + 1))) & ((ri >> sh) != (ci >> sh))
        inv_bf = [inv[h].astype(BF16) for h in heads]
        half = [_dot(inv_bf[h], l_bf[h]).astype(BF16) for h in heads]
        inv = [inv[h] - jnp.where(lower_left, _dot(half[h], inv_bf[h]), 0.0) for h in heads]
        blk *= 2

    v_new = [_dot(inv[h], rhs[h]) for h in heads]
    v_att = [narrow(v_new[h]) for h in heads]
    v_dec = [narrow(v_new[h] * e_dec[h]) for h in heads]
    for u, (b, h) in enumerate(units):
        for s in range(nseq):
            rows = slice(s * seq_len, (s + 1) * seq_len)
            so_ref[sid(b, s), h] = (state_in(b, s, h) * carry[u][s]
                                    + _dot_tn(k_op[u][rows], v_dec[u][rows]))
    for u, (b, h) in enumerate(units):
        o = e_g[u] * q_s[u] + _dot(attn[u], v_att[u])
        d_ref[b, :, h * HEAD_V:(h + 1) * HEAD_V] = (
            o * lax.rsqrt(jnp.mean(o * o, axis=-1, keepdims=True) + EPS) * onorm_ref[...])


def _mixer(tokens, gbc, gbr, convp, poolp, s0, w, *, nb, nseq, seq_len, n_valid, pos0,
           n_groups, n_steps, row0, shared_state, layer=None, prev_ssm=None,
           extra_out_rows=0, prev_ad=None):
    assert nseq * seq_len == SLAB and (nseq == 1 or n_steps == 1)
    assert layer is None or not shared_state
    preconv = convp is None
    n_tok = gbc.shape[0]
    row_len = n_steps * SLAB
    n_rows_in = n_tok // row_len
    seq_blk = nb * nseq
    n_seq_total = n_groups * seq_blk
    tok = lambda gi, t: (row0 + gi, t, 0)
    c2 = lambda gi, t: (0, 0)
    c3 = lambda gi, t: (0, 0, 0)

    def state_specs(per_seq, stacked_like=None):
        zeros = (0,) * len(per_seq)
        blk = (seq_blk,) + per_seq
        if stacked_like is not None:
            spec = pl.BlockSpec((None,) + blk, lambda gi, t: (layer, gi) + zeros)
            return spec, spec, jax.ShapeDtypeStruct(stacked_like.shape, F32)
        in_spec = (pl.BlockSpec((1,) + per_seq, lambda gi, t: (0,) + zeros) if shared_state
                   else pl.BlockSpec(blk, lambda gi, t: (gi,) + zeros))
        out_spec = pl.BlockSpec(blk, lambda gi, t: (gi,) + zeros)
        return in_spec, out_spec, jax.ShapeDtypeStruct((n_seq_total,) + per_seq, F32)

    pool_in, pool_out, pool_shape = state_specs((POOL_STATE, POOL_WIDTH))
    s0_spec, so_spec, so_shape = state_specs((N_HEADS, HEAD_K, HEAD_V),
                                             s0 if layer is not None else None)
    widths = [a.shape[1] for a in tokens]
    in_specs = [pl.BlockSpec((nb, SLAB, wd), tok) for wd in widths]
    in_specs += [pl.BlockSpec((nb, SLAB, GB_W), tok),
                 pl.BlockSpec((nb, 1, 2 * N_HEADS, SLAB), lambda gi, t: (row0 + gi, t, 0, 0))]
    gbr_slabs = gbr.reshape(2 * N_HEADS, n_rows_in, n_steps, SLAB).transpose(1, 2, 0, 3)
    args = [a.reshape(n_rows_in, row_len, a.shape[1]) for a in tokens]
    args += [gbc.reshape(n_rows_in, row_len, GB_W), gbr_slabs]
    if not preconv:
        conv_in, conv_out, conv_shape = state_specs((CONV_W - 1, QKV_W))
        in_specs.append(conv_in)
        args.append(convp)
    in_specs += [pool_in, s0_spec]
    args += [poolp, s0]
    if not preconv:
        in_specs.append(pl.BlockSpec((CONV_W, QKV_W), c2))
        args.append(w["conv_w"])
    in_specs += [pl.BlockSpec((1, HEAD_V), c2),
                 pl.BlockSpec((POOL_GROUPS, POOL_GROUP_W, POOL_GROUP_W), c3),
                 pl.BlockSpec((1, POOL_WIDTH), c2)]
    args += [w["o_norm_g"], w["pool_w"], w["pool_scale"]]
    out_row0 = row0 if prev_ad is not None else 0
    n_rows_out = n_rows_in if prev_ad is not None else n_groups * nb + extra_out_rows
    out_tok = lambda gi, t: (out_row0 + gi, t, 0)
    out_specs = [pl.BlockSpec((nb, SLAB, POOL_WIDTH), out_tok),
                 pl.BlockSpec((nb, SLAB, VAL_W), out_tok)]
    out_shape = [jax.ShapeDtypeStruct((n_rows_out, row_len, POOL_WIDTH), F32),
                 jax.ShapeDtypeStruct((n_rows_out, row_len, VAL_W), F32)]
    scratch = []
    if not preconv:
        out_specs.append(conv_out)
        out_shape.append(conv_shape)
        scratch.append(pltpu.VMEM((seq_blk, SUBLANES + seq_len, QKV_W), F32))
    out_specs += [pool_out, so_spec]
    out_shape += [pool_shape, so_shape]
    scratch.append(pltpu.VMEM((seq_blk, 2 * SUBLANES + seq_len, POOL_WIDTH), F32))
    aliases = {}
    passthrough = []
    if prev_ad is not None:
        passthrough += [(p.reshape(n_rows_out, row_len, p.shape[1]), i)
                        for i, p in enumerate(prev_ad)]
    if prev_ssm is not None:
        passthrough.append((prev_ssm, len(out_shape) - 1))
    for arr, out_index in passthrough:
        in_specs.append(pl.BlockSpec(memory_space=pl.ANY))
        args.append(arr)
        aliases[len(args) - 1] = out_index
    outs = pl.pallas_call(
        functools.partial(_mixer_kernel, nb, nseq, seq_len, n_valid, pos0, n_steps, preconv,
                          shared_state, len(aliases)),
        grid=(n_groups, n_steps),
        in_specs=in_specs,
        out_specs=out_specs,
        out_shape=out_shape,
        scratch_shapes=scratch,
        input_output_aliases=aliases,
        compiler_params=pltpu.CompilerParams(
            dimension_semantics=("parallel", "arbitrary"), vmem_limit_bytes=VMEM_LIMIT),
        name="mixer_%dx%dx%d%s" % (nb, nseq, seq_len, "_preconv" if preconv else ""),
    )(*args)
    a_out, d_out = (o.reshape(n_rows_out * row_len, o.shape[2]) for o in outs[:2])
    return [a_out, d_out] + list(outs[2:])


def _out_ffn_kernel(x_ref, a_ref, d_ref, z_ref, gates_ref, wbp_ref, wbd_ref, wout_ref, wup_ref,
                    wdown_ref, gpm_ref, gpf_ref, gqf_ref, y_ref):
    br_pool = jnp.dot(a_ref[...].astype(BF16), wbp_ref[...], preferred_element_type=F32)
    delta = (d_ref[...] * _silu(z_ref[...])).astype(BF16)
    br_delta = jnp.dot(delta, wbd_ref[...], preferred_element_type=F32)
    gate_pool = _sigmoid(gates_ref[:, :D_MODEL])
    gate_delta = _sigmoid(gates_ref[:, D_MODEL:])
    merged = (gate_pool * br_pool + gate_delta * br_delta).astype(BF16)
    m = jnp.dot(merged, wout_ref[...], preferred_element_type=F32)
    x = x_ref[...] + _rms(m, gpm_ref[...])
    h2 = _rms(x, gpf_ref[...]).astype(BF16)
    up = jnp.dot(h2, wup_ref[...], preferred_element_type=F32)
    act = jnp.square(jnp.maximum(up, 0.0)).astype(BF16)
    f = jnp.dot(act, wdown_ref[...], preferred_element_type=F32)
    y_ref[...] = x + _rms(f, gqf_ref[...])


def _out_ffn(x, a, d, z, gates, w, tm, layer):
    n = x.shape[0]
    assert n % tm == 0
    const = lambda i: (0, 0)
    tok = lambda i: (i, 0)
    resident = pl.Buffered(1)

    def weight(rows, cols):
        return pl.BlockSpec((None, rows, cols), lambda i: (layer, 0, 0), pipeline_mode=resident)

    in_specs = [
        pl.BlockSpec((tm, D_MODEL), tok),
        pl.BlockSpec((tm, POOL_WIDTH), tok),
        pl.BlockSpec((tm, VAL_W), tok),
        pl.BlockSpec((tm, VAL_W), tok),
        pl.BlockSpec((tm, 2 * D_MODEL), tok),
        weight(POOL_WIDTH, D_MODEL),
        weight(VAL_W, D_MODEL),
        weight(D_MODEL, D_MODEL),
        weight(D_MODEL, D_FF),
        weight(D_FF, D_MODEL),
        pl.BlockSpec((1, D_MODEL), const),
        pl.BlockSpec((1, D_MODEL), const),
        pl.BlockSpec((1, D_MODEL), const),
    ]
    return pl.pallas_call(
        _out_ffn_kernel,
        grid=(n // tm,),
        in_specs=in_specs,
        out_specs=pl.BlockSpec((tm, D_MODEL), tok),
        out_shape=jax.ShapeDtypeStruct((n, D_MODEL), F32),
        compiler_params=pltpu.CompilerParams(
            dimension_semantics=("parallel",), vmem_limit_bytes=VMEM_LIMIT),
        name="out_ffn",
    )(x, a, d, z, gates, w["wbp"], w["wbd"], w["wout"], w["wup"], w["wdown"],
      w["g_post_mix"], w["g_pre_ffn"], w["g_post_ffn"])


def _layer_weights(l, g_pre_mix, w_in, conv_w, a_log, dt_bias, o_norm_g, pool_w, pool_scale,
                   w_branch_pool, w_branch_delta, w_out, g_post_mix, g_pre_ffn, w_up, w_down,
                   g_post_ffn):
    wi = w_in[l]
    o_qkv = POOL_WIDTH
    o_b = o_qkv + QKV_W
    o_z = o_b + 2 * N_HEADS
    o_g = o_z + VAL_W
    wba = lax.optimization_barrier(wi[:, o_b:o_z])
    zeros8 = jnp.zeros((N_HEADS,), F32)
    a_pad = jnp.concatenate([zeros8, a_log[l]])
    dt_pad = jnp.concatenate([zeros8, dt_bias[l]])
    params = jnp.stack([a_pad, dt_pad])
    pcol = jnp.pad(params, ((0, SUBLANES - 2), (0, GB_W - 2 * N_HEADS)))
    prow = jnp.pad(params.T, ((0, 0), (0, LANES - 2)))
    return {
        "g_pre_mix": g_pre_mix[l][None],
        "wu": wi[:, :o_qkv].astype(BF16),
        "wqkv": wi[:, o_qkv:o_b].astype(BF16),
        "wz": wi[:, o_z:o_g].astype(BF16),
        "wg": wi[:, o_g:].astype(BF16),
        "wba": jnp.pad(wba, ((0, 0), (0, GB_W - 2 * N_HEADS))).astype(BF16),
        "wbat": wba.T.astype(BF16),
        "pcol": pcol,
        "prow": prow,
        "conv_w": conv_w[l],
        "o_norm_g": o_norm_g[l][None],
        "pool_w": pool_w[l].astype(BF16),
        "pool_scale": pool_scale[l][None],
        "wbp": w_branch_pool.astype(BF16),
        "wbd": w_branch_delta.astype(BF16),
        "wout": w_out.astype(BF16),
        "wup": w_up.astype(BF16),
        "wdown": w_down.astype(BF16),
        "g_post_mix": g_post_mix[l][None],
        "g_pre_ffn": g_pre_ffn[l][None],
        "g_post_ffn": g_post_ffn[l][None],
    }


def kernel(x_prompt, x_sample, state_conv, state_ssm, state_pool, meta_tokens, g_pre_mix, w_in, conv_w, a_log, dt_bias, o_norm_g, pool_w, pool_scale, w_branch_pool, w_branch_delta, w_out, g_post_mix, g_pre_ffn, w_up, w_down, g_post_ffn):
    bp, seq, _ = x_prompt.shape
    bs, dec_seq, _ = x_sample.shape
    depth = w_in.shape[0]
    n_prompt = bp * seq
    n_sample = bs * dec_seq
    assert seq % 256 == 0 and n_sample % SLAB == 0 and SLAB % dec_seq == 0 and N_META <= SLAB
    xp = x_prompt.reshape(n_prompt, D_MODEL)
    xs = jnp.concatenate([x_sample.reshape(n_sample, D_MODEL), meta_tokens.astype(F32),
                          jnp.zeros((SLAB - N_META, D_MODEL), F32)], axis=0)
    n_small = n_sample + SLAB
    small_tm = 384 if n_small % 384 == 0 else SLAB
    meta_block = n_sample // SLAB
    zero_conv = jnp.zeros((1, CONV_W - 1, QKV_W), F32)
    zero_pool = jnp.zeros((1, POOL_STATE, POOL_WIDTH), F32)
    zero_ssm = jnp.zeros((1, N_HEADS, HEAD_K, HEAD_V), F32)
    state_ssm = state_ssm.astype(F32)
    prompt_nb = 2 if bp % 2 == 0 else 1
    outs = [[] for _ in range(5)]
    ssm_s = None
    for l in range(depth):
        w = _layer_weights(l, g_pre_mix, w_in, conv_w, a_log, dt_bias, o_norm_g, pool_w, pool_scale,
                           w_branch_pool, w_branch_delta, w_out, g_post_mix, g_pre_ffn, w_up,
                           w_down, g_post_ffn)
        us, qkvs, zs, gatess, gbcs, gbrs = _in_proj(xs, w, small_tm)
        a_s, d_s, conv_s, pool_s, ssm_s = _mixer(
            (us, qkvs), gbcs, gbrs, state_conv[l], state_pool[l], state_ssm, w,
            nb=1, nseq=SLAB // dec_seq, seq_len=dec_seq, n_valid=dec_seq, pos0=PAST_LEN,
            n_groups=n_sample // SLAB, n_steps=1, row0=0, shared_state=False,
            layer=l, prev_ssm=ssm_s, extra_out_rows=1)
        a_small, d_small, conv_m, pool_m, ssm_m = _mixer(
            (us, qkvs), gbcs, gbrs, zero_conv, zero_pool, zero_ssm, w,
            nb=1, nseq=1, seq_len=SLAB, n_valid=N_META, pos0=0, n_groups=1, n_steps=1,
            row0=meta_block, shared_state=True, prev_ad=(a_s, d_s))
        up, qnp, knp, vp, zp, gatesp, gbcp, gbrp, conv_p = _in_proj(
            xp, w, 256, conv_prefix=conv_m, n_seq=bp)
        a_p, d_p, pool_p, ssm_p = _mixer(
            (up, qnp, knp, vp), gbcp, gbrp, None, pool_m, ssm_m, w,
            nb=prompt_nb, nseq=1, seq_len=SLAB, n_valid=SLAB, pos0=N_META,
            n_groups=bp // prompt_nb, n_steps=seq // SLAB, row0=0, shared_state=True)
        xp = _out_ffn(xp, a_p, d_p, zp, gatesp, w, 256, l)
        xs = _out_ffn(xs, a_small, d_small, zs, gatess, w, small_tm, l)
        for acc, val in zip(outs, (conv_p, ssm_p, pool_p, conv_s, pool_s)):
            acc.append(val)
    y_prompt = xp.reshape(bp, seq, D_MODEL)
    y_sample = xs[:n_sample].reshape(bs, dec_seq, D_MODEL)
    conv_p, ssm_p, pool_p, conv_s, pool_s = (jnp.stack(o) for o in outs)
    return (y_prompt, y_sample, conv_p, ssm_p, pool_p, conv_s, ssm_s, pool_s)
```

```python
import functools

import jax
import jax.numpy as jnp
from jax import lax
from jax.experimental import pallas as pl
from jax.experimental.pallas import tpu as pltpu

D_MODEL = 1024
N_META = 16
POOL_GROUPS = 4
POOL_GROUP_W = 128
POOL_WIDTH = 512
POOL_WINDOWS = (2, 4, 8, 16)
POOL_STATE = 15
HEAD_K = 128
HEAD_V = 128
N_HEADS = 8
KEY_W = 1024
VAL_W = 1024
QKV_W = 3072
CONV_W = 4
D_FF = 4096
PAST_LEN = 16384
EPS = 1e-6

LANES = 128
SUBLANES = 8
SLAB = 128
PROJ_CHUNK = 256
GB_W = 128
VMEM_LIMIT = 56 * 1024 * 1024

F32 = jnp.float32
BF16 = jnp.bfloat16
NEG_BIG = -1e30
NEG_LOG2_E = -1.4426950408889634


def _dot(a, b):
    return jnp.dot(a.astype(BF16), b.astype(BF16), preferred_element_type=F32)


def _dot_nt(a, b):
    return lax.dot_general(a.astype(BF16), b.astype(BF16), (((1,), (1,)), ((), ())),
                           preferred_element_type=F32)


def _dot_tn(a, b):
    return lax.dot_general(a.astype(BF16), b.astype(BF16), (((0,), (0,)), ((), ())),
                           preferred_element_type=F32)


def _bf16_pieces(x):
    hi = x.astype(BF16)
    rest = x - hi.astype(F32)
    mid = rest.astype(BF16)
    lo = (rest - mid.astype(F32)).astype(BF16)
    return hi, mid, lo


def _dot_exact_mask(mask, pieces):
    m = mask.astype(F32).astype(BF16)
    hi, mid, lo = (jnp.dot(m, p, preferred_element_type=F32) for p in pieces)
    return hi + mid + lo


def _rms(x, g):
    return x * lax.rsqrt(jnp.mean(x * x, axis=-1, keepdims=True) + EPS) * g


def _sigmoid(x):
    return 1.0 / (1.0 + jnp.exp2(x * NEG_LOG2_E))


def _silu(x):
    return x * _sigmoid(x)


def _softplus(x):
    return jnp.maximum(x, 0.0) + jnp.log1p(jnp.exp(-jnp.abs(x)))


def _l2_normalize(x, scale=1.0):
    return x * (lax.rsqrt(jnp.sum(x * x, axis=-1, keepdims=True) + EPS) * scale)


def _causal_conv_silu(ext_ref, seq, row0, n_rows, lanes, convw_ref):
    if seq is None and row0 >= SUBLANES and row0 % SUBLANES == 0 and n_rows % SUBLANES == 0:
        full = ext_ref[row0 - SUBLANES:row0 + n_rows, lanes]
        acc = full[SUBLANES:] * convw_ref[CONV_W - 1:CONV_W, lanes]
        for d in range(1, CONV_W):
            tap = pltpu.roll(full, d, axis=0)[SUBLANES:]
            acc = acc + tap * convw_ref[CONV_W - 1 - d:CONV_W - d, lanes]
        return _silu(acc)
    acc = None
    for j in range(CONV_W):
        lo = row0 - (CONV_W - 1) + j
        rows = ext_ref[lo:lo + n_rows, lanes] if seq is None else ext_ref[seq, lo:lo + n_rows, lanes]
        term = rows * convw_ref[j:j + 1, lanes]
        acc = term if acc is None else acc + term
    return _silu(acc)


def _beta_decay_cols(ba, pcol_ref):
    lane = lax.broadcasted_iota(jnp.int32, ba.shape, 1)
    g_col = -jnp.exp(pcol_ref[0:1, :]) * _softplus(ba + pcol_ref[1:2, :])
    return jnp.where(lane < N_HEADS, _sigmoid(ba), jnp.where(lane < 2 * N_HEADS, g_col, 0.0))


def _in_proj_kernel(x_ref, g_ref, wu_ref, wqkv_ref, wz_ref, wg_ref, wba_ref, pcol_ref,
                    u_ref, qkv_ref, z_ref, gates_ref, gbc_ref):
    h = _rms(x_ref[...], g_ref[...]).astype(BF16)
    u_ref[...] = jnp.dot(h, wu_ref[...], preferred_element_type=F32)
    qkv_ref[...] = jnp.dot(h, wqkv_ref[...], preferred_element_type=F32)
    z_ref[...] = jnp.dot(h, wz_ref[...], preferred_element_type=F32)
    gates_ref[...] = jnp.dot(h, wg_ref[...], preferred_element_type=F32)
    gbc_ref[...] = _beta_decay_cols(jnp.dot(h, wba_ref[...], preferred_element_type=F32), pcol_ref)


def _in_proj_conv_kernel(tm, n_tiles,
                         x_ref, g_ref, wu_ref, wqkv_ref, wz_ref, wg_ref, wba_ref, pcol_ref,
                         convw_ref, convp_ref,
                         u_ref, qn_ref, kn_ref, v_ref, z_ref, gates_ref, gbc_ref, convo_ref,
                         ext):
    t = pl.program_id(1)
    base = SUBLANES
    if n_tiles > 1:
        @pl.when(t > 0)
        def _():
            ext[base - (CONV_W - 1):base, :] = ext[base + tm - (CONV_W - 1):base + tm, :]

    @pl.when(t == 0)
    def _():
        ext[base - (CONV_W - 1):base, :] = convp_ref[0]

    h = _rms(x_ref[...], g_ref[...]).astype(BF16)
    u_ref[...] = jnp.dot(h, wu_ref[...], preferred_element_type=F32)
    n_groups = QKV_W // LANES
    chunks = ([(z_ref, wz_ref, c0) for c0 in range(0, VAL_W, PROJ_CHUNK)]
              + [(gates_ref, wg_ref, c0) for c0 in range(0, 2 * D_MODEL, PROJ_CHUNK)])
    assert n_groups % len(chunks) == 0
    every = n_groups // len(chunks)
    groups_per_chunk = PROJ_CHUNK // LANES
    for c in range(n_groups):
        if c % groups_per_chunk == 0:
            cols = slice(c * LANES, c * LANES + PROJ_CHUNK)
            ext[base:base + tm, cols] = jnp.dot(h, wqkv_ref[:, cols], preferred_element_type=F32)
            convo_ref[0, :, cols] = ext[base + tm - (CONV_W - 1):base + tm, cols]
        lanes = slice(c * LANES, (c + 1) * LANES)
        head_lanes = slice((c % N_HEADS) * LANES, (c % N_HEADS + 1) * LANES)
        y = _causal_conv_silu(ext, None, base, tm, lanes, convw_ref)
        if c < N_HEADS:
            qn_ref[:, head_lanes] = _l2_normalize(y, HEAD_K ** -0.5)
        elif c < 2 * N_HEADS:
            kn_ref[:, head_lanes] = _l2_normalize(y)
        else:
            v_ref[:, head_lanes] = y
        if c % every == 0:
            out_ref, w_ref, c0 = chunks[c // every]
            out_ref[:, c0:c0 + PROJ_CHUNK] = jnp.dot(h, w_ref[:, c0:c0 + PROJ_CHUNK],
                                                     preferred_element_type=F32)
    gbc_ref[...] = _beta_decay_cols(jnp.dot(h, wba_ref[...], preferred_element_type=F32), pcol_ref)


def _in_proj(x, w, tm, conv_prefix=None, n_seq=1):
    n = x.shape[0]
    fused = conv_prefix is not None
    assert n % (tm * n_seq) == 0
    n_tiles = n // (tm * n_seq)
    if fused:
        grid = (n_seq, n_tiles)
        const = lambda s, t: (0, 0)
        tok = lambda s, t: (s * n_tiles + t, 0)
    else:
        grid = (n_tiles,)
        const = lambda i: (0, 0)
        tok = lambda i: (i, 0)
    resident = pl.Buffered(1)
    in_specs = [
        pl.BlockSpec((tm, D_MODEL), tok),
        pl.BlockSpec((1, D_MODEL), const),
        pl.BlockSpec((D_MODEL, POOL_WIDTH), const, pipeline_mode=resident),
        pl.BlockSpec((D_MODEL, QKV_W), const, pipeline_mode=resident),
        pl.BlockSpec((D_MODEL, VAL_W), const, pipeline_mode=resident),
        pl.BlockSpec((D_MODEL, 2 * D_MODEL), const, pipeline_mode=resident),
        pl.BlockSpec((D_MODEL, GB_W), const, pipeline_mode=resident),
        pl.BlockSpec((SUBLANES, GB_W), const),
    ]
    args = [x, w["g_pre_mix"], w["wu"], w["wqkv"], w["wz"], w["wg"], w["wba"], w["pcol"]]
    tail_specs = [
        pl.BlockSpec((tm, 2 * D_MODEL), tok),
        pl.BlockSpec((tm, GB_W), tok),
    ]
    tail_shapes = [
        jax.ShapeDtypeStruct((n, 2 * D_MODEL), F32),
        jax.ShapeDtypeStruct((n, GB_W), F32),
    ]
    if fused:
        in_specs += [pl.BlockSpec((CONV_W, QKV_W), const),
                     pl.BlockSpec((1, CONV_W - 1, QKV_W), lambda s, t: (0, 0, 0))]
        args += [w["conv_w"], conv_prefix]
        out_specs = [
            pl.BlockSpec((tm, POOL_WIDTH), tok),
            pl.BlockSpec((tm, KEY_W), tok),
            pl.BlockSpec((tm, KEY_W), tok),
            pl.BlockSpec((tm, VAL_W), tok),
            pl.BlockSpec((tm, VAL_W), tok),
        ] + tail_specs + [pl.BlockSpec((1, CONV_W - 1, QKV_W), lambda s, t: (s, 0, 0))]
        out_shape = [
            jax.ShapeDtypeStruct((n, POOL_WIDTH), F32),
            jax.ShapeDtypeStruct((n, KEY_W), F32),
            jax.ShapeDtypeStruct((n, KEY_W), F32),
            jax.ShapeDtypeStruct((n, VAL_W), F32),
            jax.ShapeDtypeStruct((n, VAL_W), F32),
        ] + tail_shapes + [jax.ShapeDtypeStruct((n_seq, CONV_W - 1, QKV_W), F32)]
        body = functools.partial(_in_proj_conv_kernel, tm, n_tiles)
        scratch = [pltpu.VMEM((SUBLANES + tm, QKV_W), F32)]
        semantics = ("parallel", "arbitrary")
    else:
        out_specs = [
            pl.BlockSpec((tm, POOL_WIDTH), tok),
            pl.BlockSpec((tm, QKV_W), tok),
            pl.BlockSpec((tm, VAL_W), tok),
        ] + tail_specs
        out_shape = [
            jax.ShapeDtypeStruct((n, POOL_WIDTH), F32),
            jax.ShapeDtypeStruct((n, QKV_W), F32),
            jax.ShapeDtypeStruct((n, VAL_W), F32),
        ] + tail_shapes
        body = _in_proj_kernel
        scratch = []
        semantics = ("parallel",)
    return pl.pallas_call(
        body,
        grid=grid,
        in_specs=in_specs,
        out_specs=out_specs,
        out_shape=out_shape,
        scratch_shapes=scratch,
        compiler_params=pltpu.CompilerParams(
            dimension_semantics=semantics, vmem_limit_bytes=VMEM_LIMIT),
        name="in_proj_conv" if fused else "in_proj",
    )(*args)


def _mixer_kernel(nb, nseq, seq_len, n_valid, pos0, n_steps, preconv, shared_state, n_passthrough,
                  *refs):
    refs = list(refs)
    rest = refs[10 + n_passthrough:]
    if preconv:
        (u_ref, qn_ref, kn_ref, v_ref, gbc_ref, poolp_ref, s0_ref,
         onorm_ref, poolw_ref, pscale_ref) = refs[:10]
        a_ref, d_ref, poolo_ref, so_ref, u_ext = rest
    else:
        (u_ref, qkv_ref, gbc_ref, convp_ref, poolp_ref, s0_ref,
         convw_ref, onorm_ref, poolw_ref, pscale_ref) = refs[:10]
        a_ref, d_ref, convo_ref, poolo_ref, so_ref, qkv_ext, u_ext = rest
    t = pl.program_id(1)
    conv_base = SUBLANES
    pool_base = 2 * SUBLANES
    last = n_valid if n_valid < seq_len else seq_len
    seqs = [(b, s) for b in range(nb) for s in range(nseq)]
    sid = lambda b, s: b * nseq + s
    src = lambda b, s: 0 if shared_state else sid(b, s)

    if n_steps > 1:
        @pl.when(t > 0)
        def _():
            for b, s in seqs:
                i = sid(b, s)
                u_ext[i, pool_base - POOL_STATE:pool_base, :] = (
                    u_ext[i, pool_base + seq_len - POOL_STATE:pool_base + seq_len, :])
                if not preconv:
                    qkv_ext[i, conv_base - (CONV_W - 1):conv_base, :] = (
                        qkv_ext[i, conv_base + seq_len - (CONV_W - 1):conv_base + seq_len, :])

    @pl.when(t == 0)
    def _():
        for b, s in seqs:
            i = sid(b, s)
            u_ext[i, pool_base - POOL_STATE:pool_base, :] = poolp_ref[src(b, s)]
            if not preconv:
                qkv_ext[i, conv_base - (CONV_W - 1):conv_base, :] = convp_ref[src(b, s)]
            if n_steps > 1:
                so_ref[i] = s0_ref[src(b, s)]

    def state_in(b, s, h):
        return so_ref[sid(b, s), h] if n_steps > 1 else s0_ref[src(b, s), h]

    for b, s in seqs:
        i = sid(b, s)
        rows = slice(s * seq_len, (s + 1) * seq_len)
        u_ext[i, pool_base:pool_base + seq_len, :] = u_ref[b, rows, :]
        poolo_ref[i] = u_ext[i, pool_base + last - POOL_STATE:pool_base + last, :]
        if not preconv:
            qkv_ext[i, conv_base:conv_base + seq_len, :] = qkv_ref[b, rows, :]
            convo_ref[i] = qkv_ext[i, conv_base + last - (CONV_W - 1):conv_base + last, :]

    for b in range(nb):
        for gi, win in enumerate(POOL_WINDOWS):
            lanes = slice(gi * POOL_GROUP_W, (gi + 1) * POOL_GROUP_W)
            if pos0 >= POOL_STATE:
                cnt = float(win)
            else:
                pos = lax.broadcasted_iota(jnp.int32, (seq_len, 1), 0) + (pos0 + 1) + t * seq_len
                cnt = jnp.minimum(pos, win).astype(F32)
            pieces = []
            for s in range(nseq):
                i = sid(b, s)
                tok = u_ext[i, pool_base:pool_base + seq_len, lanes]
                tot = tok
                for j in range(1, win):
                    tot = tot + u_ext[i, pool_base - j:pool_base - j + seq_len, lanes]
                pieces.append(tot / cnt - tok)
            dev = pieces[0] if nseq == 1 else jnp.concatenate(pieces, axis=0)
            a_ref[b, :, lanes] = _dot(dev, poolw_ref[gi]) * pscale_ref[:, lanes]

    ri = lax.broadcasted_iota(jnp.int32, (SLAB, SLAB), 0)
    ci = lax.broadcasted_iota(jnp.int32, (SLAB, SLAB), 1)
    shift = seq_len.bit_length() - 1
    same = (ri >> shift) == (ci >> shift)
    m_le = same & (ci <= ri)
    m_lt = same & (ci < ri)
    gbc, gc_col, gc_row, gl_col = [], [], [], []
    for b in range(nb):
        gbc_b = gbc_ref[b]
        if n_valid < seq_len:
            gbc_b = jnp.where(lax.broadcasted_iota(jnp.int32, gbc_b.shape, 0) < n_valid, gbc_b, 0.0)
        gbc.append(gbc_b)
        pieces = _bf16_pieces(gbc_b)
        gc_b = _dot_exact_mask(m_le, pieces)
        gc_col.append(gc_b)
        gc_row.append(gc_b.T)
        if nseq == 1:
            gl_col.append(gc_b[SLAB - 1:SLAB, :])
        else:
            gl_col.append(_dot_exact_mask(same, pieces))

    units = [(b, h) for b in range(nb) for h in range(N_HEADS)]
    heads = range(len(units))
    eye = (ri == ci).astype(F32)
    narrow = (lambda a: a.astype(BF16)) if seq_len % (2 * SUBLANES) == 0 else (lambda a: a)

    def conv(b, lanes):
        pieces = [_causal_conv_silu(qkv_ext, sid(b, s), conv_base, seq_len, lanes, convw_ref)
                  for s in range(nseq)]
        return pieces[0] if nseq == 1 else jnp.concatenate(pieces, axis=0)

    kq, k_op, v, beta, e_g, e_dec, carry, decay_le, decay_lb = [], [], [], [], [], [], [], [], []
    for b, h in units:
        hl = slice(h * HEAD_K, (h + 1) * HEAD_K)
        if preconv:
            q, k = narrow(qn_ref[b, :, hl]), narrow(kn_ref[b, :, hl])
            v.append(v_ref[b, :, hl])
        else:
            q = narrow(_l2_normalize(conv(b, hl), HEAD_K ** -0.5))
            k = narrow(_l2_normalize(conv(b, slice(KEY_W + h * HEAD_K, KEY_W + (h + 1) * HEAD_K))))
            v.append(conv(b, slice(2 * KEY_W + h * HEAD_V, 2 * KEY_W + (h + 1) * HEAD_V)))
        k_op.append(k)
        kq.append(jnp.concatenate([k, q], axis=0))

    kk_qk = [_dot_nt(kq[h], k_op[h]) for h in heads]

    k_s, q_s = [], []
    for u, (b, h) in enumerate(units):
        if nseq == 1:
            prod = _dot(kq[u], state_in(b, 0, h))
            k_s.append(prod[:SLAB])
            q_s.append(prod[SLAB:])
        else:
            ks_parts, qs_parts = [], []
            for s in range(nseq):
                rows = slice(s * seq_len, (s + 1) * seq_len)
                pair_rows = jnp.concatenate(
                    [kq[u][rows], kq[u][SLAB + s * seq_len:SLAB + (s + 1) * seq_len]], axis=0)
                prod = _dot(pair_rows, state_in(b, s, h))
                ks_parts.append(prod[:seq_len])
                qs_parts.append(prod[seq_len:])
            k_s.append(jnp.concatenate(ks_parts, axis=0))
            q_s.append(jnp.concatenate(qs_parts, axis=0))

    for b, h in units:
        bt = gbc[b][:, h:h + 1]
        g_c = gc_col[b][:, N_HEADS + h:N_HEADS + h + 1]
        g_r = gc_row[b][N_HEADS + h:N_HEADS + h + 1, :]
        g_l = gl_col[b][:, N_HEADS + h:N_HEADS + h + 1]
        d_le = jnp.exp(jnp.where(m_le, g_c - g_r, NEG_BIG))
        decay_le.append(d_le)
        decay_lb.append(jnp.where(m_lt, d_le, 0.0) * bt)
        beta.append(bt)
        e_g.append(jnp.exp(g_c))
        e_dec.append(jnp.exp(g_l - g_c))
        carry.append([jnp.exp(g_l[s * seq_len:s * seq_len + 1, :]) for s in range(nseq)])

    lmat = [kk_qk[h][:SLAB] * decay_lb[h] for h in heads]
    attn = [(kk_qk[h][SLAB:] * decay_le[h]).astype(BF16) for h in heads]
    rhs =[(beta[h] * (v[h] - e_g[h] * k_s[h])).astype(BF16) for h in heads]

    pair = (ri >> 1) == (ci >> 1)
    l_bf = [lmat[h].astype(BF16) for h in heads]
    inv = [eye - jnp.where(pair, lmat[h], 0.0) for h in heads]
    blk = 2
    while blk < seq_len:
        sh = blk.bit_length() - 1
        lower_left = ((ri >> (sh + 1)) == (ci >> (sh + 1))) & ((ri >> sh) != (ci >> sh))
        inv_bf = [inv[h].astype(BF16) for h in heads]
        half = [_dot(inv_bf[h], l_bf[h]).astype(BF16) for h in heads]
        inv = [inv[h] - jnp.where(lower_left, _dot(half[h], inv_bf[h]), 0.0) for h in heads]
        blk *= 2

    v_new = [_dot(inv[h], rhs[h]) for h in heads]
    v_att = [narrow(v_new[h]) for h in heads]
    v_dec = [narrow(v_new[h] * e_dec[h]) for h in heads]
    for u, (b, h) in enumerate(units):
        for s in range(nseq):
            rows = slice(s * seq_len, (s + 1) * seq_len)
            so_ref[sid(b, s), h] = (state_in(b, s, h) * carry[u][s]
                                    + _dot_tn(k_op[u][rows], v_dec[u][rows]))
    for u, (b, h) in enumerate(units):
        o = e_g[u] * q_s[u] + _dot(attn[u], v_att[u])
        d_ref[b, :, h * HEAD_V:(h + 1) * HEAD_V] = (
            o * lax.rsqrt(jnp.mean(o * o, axis=-1, keepdims=True) + EPS) * onorm_ref[...])


def _mixer(tokens, gbc, convp, poolp, s0, w, *, nb, nseq, seq_len, n_valid, pos0,
           n_groups, n_steps, row0, shared_state, layer=None, prev_ssm=None,
           extra_out_rows=0, prev_ad=None):
    assert nseq * seq_len == SLAB and (nseq == 1 or n_steps == 1)
    assert layer is None or not shared_state
    preconv = convp is None
    n_tok = gbc.shape[0]
    row_len = n_steps * SLAB
    n_rows_in = n_tok // row_len
    seq_blk = nb * nseq
    n_seq_total = n_groups * seq_blk
    tok = lambda gi, t: (row0 + gi, t, 0)
    c2 = lambda gi, t: (0, 0)
    c3 = lambda gi, t: (0, 0, 0)

    def state_specs(per_seq, stacked_like=None):
        zeros = (0,) * len(per_seq)
        blk = (seq_blk,) + per_seq
        if stacked_like is not None:
            spec = pl.BlockSpec((None,) + blk, lambda gi, t: (layer, gi) + zeros)
            return spec, spec, jax.ShapeDtypeStruct(stacked_like.shape, F32)
        in_spec = (pl.BlockSpec((1,) + per_seq, lambda gi, t: (0,) + zeros) if shared_state
                   else pl.BlockSpec(blk, lambda gi, t: (gi,) + zeros))
        out_spec = pl.BlockSpec(blk, lambda gi, t: (gi,) + zeros)
        return in_spec, out_spec, jax.ShapeDtypeStruct((n_seq_total,) + per_seq, F32)

    pool_in, pool_out, pool_shape = state_specs((POOL_STATE, POOL_WIDTH))
    s0_spec, so_spec, so_shape = state_specs((N_HEADS, HEAD_K, HEAD_V),
                                             s0 if layer is not None else None)
    widths = [a.shape[1] for a in tokens]
    in_specs = [pl.BlockSpec((nb, SLAB, wd), tok) for wd in widths]
    in_specs.append(pl.BlockSpec((nb, SLAB, GB_W), tok))
    args = [a.reshape(n_rows_in, row_len, a.shape[1]) for a in tokens]
    args.append(gbc.reshape(n_rows_in, row_len, GB_W))
    if not preconv:
        conv_in, conv_out, conv_shape = state_specs((CONV_W - 1, QKV_W))
        in_specs.append(conv_in)
        args.append(convp)
    in_specs += [pool_in, s0_spec]
    args += [poolp, s0]
    if not preconv:
        in_specs.append(pl.BlockSpec((CONV_W, QKV_W), c2))
        args.append(w["conv_w"])
    in_specs += [pl.BlockSpec((1, HEAD_V), c2),
                 pl.BlockSpec((POOL_GROUPS, POOL_GROUP_W, POOL_GROUP_W), c3),
                 pl.BlockSpec((1, POOL_WIDTH), c2)]
    args += [w["o_norm_g"], w["pool_w"], w["pool_scale"]]
    out_row0 = row0 if prev_ad is not None else 0
    n_rows_out = n_rows_in if prev_ad is not None else n_groups * nb + extra_out_rows
    out_tok = lambda gi, t: (out_row0 + gi, t, 0)
    out_specs = [pl.BlockSpec((nb, SLAB, POOL_WIDTH), out_tok),
                 pl.BlockSpec((nb, SLAB, VAL_W), out_tok)]
    out_shape = [jax.ShapeDtypeStruct((n_rows_out, row_len, POOL_WIDTH), F32),
                 jax.ShapeDtypeStruct((n_rows_out, row_len, VAL_W), F32)]
    scratch = []
    if not preconv:
        out_specs.append(conv_out)
        out_shape.append(conv_shape)
        scratch.append(pltpu.VMEM((seq_blk, SUBLANES + seq_len, QKV_W), F32))
    out_specs += [pool_out, so_spec]
    out_shape += [pool_shape, so_shape]
    scratch.append(pltpu.VMEM((seq_blk, 2 * SUBLANES + seq_len, POOL_WIDTH), F32))
    aliases = {}
    passthrough = []
    if prev_ad is not None:
        passthrough += [(p.reshape(n_rows_out, row_len, p.shape[1]), i)
                        for i, p in enumerate(prev_ad)]
    if prev_ssm is not None:
        passthrough.append((prev_ssm, len(out_shape) - 1))
    for arr, out_index in passthrough:
        in_specs.append(pl.BlockSpec(memory_space=pl.ANY))
        args.append(arr)
        aliases[len(args) - 1] = out_index
    outs = pl.pallas_call(
        functools.partial(_mixer_kernel, nb, nseq, seq_len, n_valid, pos0, n_steps, preconv,
                          shared_state, len(aliases)),
        grid=(n_groups, n_steps),
        in_specs=in_specs,
        out_specs=out_specs,
        out_shape=out_shape,
        scratch_shapes=scratch,
        input_output_aliases=aliases,
        compiler_params=pltpu.CompilerParams(
            dimension_semantics=("parallel", "arbitrary"), vmem_limit_bytes=VMEM_LIMIT),
        name="mixer_%dx%dx%d%s" % (nb, nseq, seq_len, "_preconv" if preconv else ""),
    )(*args)
    a_out, d_out = (o.reshape(n_rows_out * row_len, o.shape[2]) for o in outs[:2])
    return [a_out, d_out] + list(outs[2:])


def _out_ffn_kernel(x_ref, a_ref, d_ref, z_ref, gates_ref, wbp_ref, wbd_ref, wout_ref, wup_ref,
                    wdown_ref, gpm_ref, gpf_ref, gqf_ref, y_ref):
    br_pool = jnp.dot(a_ref[...].astype(BF16), wbp_ref[...], preferred_element_type=F32)
    delta = (d_ref[...] * _silu(z_ref[...])).astype(BF16)
    br_delta = jnp.dot(delta, wbd_ref[...], preferred_element_type=F32)
    gate_pool = _sigmoid(gates_ref[:, :D_MODEL])
    gate_delta = _sigmoid(gates_ref[:, D_MODEL:])
    merged = (gate_pool * br_pool + gate_delta * br_delta).astype(BF16)
    m = jnp.dot(merged, wout_ref[...], preferred_element_type=F32)
    x = x_ref[...] + _rms(m, gpm_ref[...])
    h2 = _rms(x, gpf_ref[...]).astype(BF16)
    up = jnp.dot(h2, wup_ref[...], preferred_element_type=F32)
    act = jnp.square(jnp.maximum(up, 0.0)).astype(BF16)
    f = jnp.dot(act, wdown_ref[...], preferred_element_type=F32)
    y_ref[...] = x + _rms(f, gqf_ref[...])


def _out_ffn(x, a, d, z, gates, w, tm, layer):
    n = x.shape[0]
    assert n % tm == 0
    const = lambda i: (0, 0)
    tok = lambda i: (i, 0)
    resident = pl.Buffered(1)

    def weight(rows, cols):
        return pl.BlockSpec((None, rows, cols), lambda i: (layer, 0, 0), pipeline_mode=resident)

    in_specs = [
        pl.BlockSpec((tm, D_MODEL), tok),
        pl.BlockSpec((tm, POOL_WIDTH), tok),
        pl.BlockSpec((tm, VAL_W), tok),
        pl.BlockSpec((tm, VAL_W), tok),
        pl.BlockSpec((tm, 2 * D_MODEL), tok),
        weight(POOL_WIDTH, D_MODEL),
        weight(VAL_W, D_MODEL),
        weight(D_MODEL, D_MODEL),
        weight(D_MODEL, D_FF),
        weight(D_FF, D_MODEL),
        pl.BlockSpec((1, D_MODEL), const),
        pl.BlockSpec((1, D_MODEL), const),
        pl.BlockSpec((1, D_MODEL), const),
    ]
    return pl.pallas_call(
        _out_ffn_kernel,
        grid=(n // tm,),
        in_specs=in_specs,
        out_specs=pl.BlockSpec((tm, D_MODEL), tok),
        out_shape=jax.ShapeDtypeStruct((n, D_MODEL), F32),
        compiler_params=pltpu.CompilerParams(
            dimension_semantics=("parallel",), vmem_limit_bytes=VMEM_LIMIT),
        name="out_ffn",
    )(x, a, d, z, gates, w["wbp"], w["wbd"], w["wout"], w["wup"], w["wdown"],
      w["g_post_mix"], w["g_pre_ffn"], w["g_post_ffn"])


def _layer_weights(l, g_pre_mix, w_in, conv_w, a_log, dt_bias, o_norm_g, pool_w, pool_scale,
                   w_branch_pool, w_branch_delta, w_out, g_post_mix, g_pre_ffn, w_up, w_down,
                   g_post_ffn):
    wi = w_in[l]
    o_qkv = POOL_WIDTH
    o_b = o_qkv + QKV_W
    o_z = o_b + 2 * N_HEADS
    o_g = o_z + VAL_W
    wba = wi[:, o_b:o_z]
    zeros8 = jnp.zeros((N_HEADS,), F32)
    a_pad = jnp.concatenate([zeros8, a_log[l]])
    dt_pad = jnp.concatenate([zeros8, dt_bias[l]])
    params = jnp.stack([a_pad, dt_pad])
    pcol = jnp.pad(params, ((0, SUBLANES - 2), (0, GB_W - 2 * N_HEADS)))
    return {
        "g_pre_mix": g_pre_mix[l][None],
        "wu": wi[:, :o_qkv].astype(BF16),
        "wqkv": wi[:, o_qkv:o_b].astype(BF16),
        "wz": wi[:, o_z:o_g].astype(BF16),
        "wg": wi[:, o_g:].astype(BF16),
        "wba": jnp.pad(wba, ((0, 0), (0, GB_W - 2 * N_HEADS))).astype(BF16),
        "pcol": pcol,
        "conv_w": conv_w[l],
        "o_norm_g": o_norm_g[l][None],
        "pool_w": pool_w[l].astype(BF16),
        "pool_scale": pool_scale[l][None],
        "wbp": w_branch_pool.astype(BF16),
        "wbd": w_branch_delta.astype(BF16),
        "wout": w_out.astype(BF16),
        "wup": w_up.astype(BF16),
        "wdown": w_down.astype(BF16),
        "g_post_mix": g_post_mix[l][None],
        "g_pre_ffn": g_pre_ffn[l][None],
        "g_post_ffn": g_post_ffn[l][None],
    }


def kernel(x_prompt, x_sample, state_conv, state_ssm, state_pool, meta_tokens, g_pre_mix, w_in, conv_w, a_log, dt_bias, o_norm_g, pool_w, pool_scale, w_branch_pool, w_branch_delta, w_out, g_post_mix, g_pre_ffn, w_up, w_down, g_post_ffn):
    bp, seq, _ = x_prompt.shape
    bs, dec_seq, _ = x_sample.shape
    depth = w_in.shape[0]
    n_prompt = bp * seq
    n_sample = bs * dec_seq
    assert seq % 256 == 0 and n_sample % SLAB == 0 and SLAB % dec_seq == 0 and N_META <= SLAB
    xp = x_prompt.reshape(n_prompt, D_MODEL)
    xs = jnp.concatenate([x_sample.reshape(n_sample, D_MODEL), meta_tokens.astype(F32),
                          jnp.zeros((SLAB - N_META, D_MODEL), F32)], axis=0)
    n_small = n_sample + SLAB
    small_tm = 384 if n_small % 384 == 0 else SLAB
    meta_block = n_sample // SLAB
    zero_conv = jnp.zeros((1, CONV_W - 1, QKV_W), F32)
    zero_pool = jnp.zeros((1, POOL_STATE, POOL_WIDTH), F32)
    zero_ssm = jnp.zeros((1, N_HEADS, HEAD_K, HEAD_V), F32)
    state_ssm = state_ssm.astype(F32)
    prompt_nb = 2 if bp % 2 == 0 else 1
    outs = [[] for _ in range(5)]
    ssm_s = None
    for l in range(depth):
        w = _layer_weights(l, g_pre_mix, w_in, conv_w, a_log, dt_bias, o_norm_g, pool_w, pool_scale,
                           w_branch_pool, w_branch_delta, w_out, g_post_mix, g_pre_ffn, w_up,
                           w_down, g_post_ffn)
        us, qkvs, zs, gatess, gbcs = _in_proj(xs, w, small_tm)
        a_s, d_s, conv_s, pool_s, ssm_s = _mixer(
            (us, qkvs), gbcs, state_conv[l], state_pool[l], state_ssm, w,
            nb=1, nseq=SLAB // dec_seq, seq_len=dec_seq, n_valid=dec_seq, pos0=PAST_LEN,
            n_groups=n_sample // SLAB, n_steps=1, row0=0, shared_state=False,
            layer=l, prev_ssm=ssm_s, extra_out_rows=1)
        a_small, d_small, conv_m, pool_m, ssm_m = _mixer(
            (us, qkvs), gbcs, zero_conv, zero_pool, zero_ssm, w,
            nb=1, nseq=1, seq_len=SLAB, n_valid=N_META, pos0=0, n_groups=1, n_steps=1,
            row0=meta_block, shared_state=True, prev_ad=(a_s, d_s))
        up, qnp, knp, vp, zp, gatesp, gbcp, conv_p = _in_proj(
            xp, w, 256, conv_prefix=conv_m, n_seq=bp)
        a_p, d_p, pool_p, ssm_p = _mixer(
            (up, qnp, knp, vp), gbcp, None, pool_m, ssm_m, w,
            nb=prompt_nb, nseq=1, seq_len=SLAB, n_valid=SLAB, pos0=N_META,
            n_groups=bp // prompt_nb, n_steps=seq // SLAB, row0=0, shared_state=True)
        xp = _out_ffn(xp, a_p, d_p, zp, gatesp, w, 256, l)
        xs = _out_ffn(xs, a_small, d_small, zs, gatess, w, small_tm, l)
        for acc, val in zip(outs, (conv_p, ssm_p, pool_p, conv_s, pool_s)):
            acc.append(val)
    y_prompt = xp.reshape(bp, seq, D_MODEL)
    y_sample = xs[:n_sample].reshape(bs, dec_seq, D_MODEL)
    conv_p, ssm_p, pool_p, conv_s, pool_s = (jnp.stack(o) for o in outs)
    return (y_prompt, y_sample, conv_p, ssm_p, pool_p, conv_s, ssm_s, pool_s)
```

```python
import functools

import jax
import jax.numpy as jnp
from jax import lax
from jax.experimental import pallas as pl
from jax.experimental.pallas import tpu as pltpu

D_MODEL = 1024
N_META = 16
POOL_GROUPS = 4
POOL_GROUP_W = 128
POOL_WIDTH = 512
POOL_WINDOWS = (2, 4, 8, 16)
POOL_STATE = 15
HEAD_K = 128
HEAD_V = 128
N_HEADS = 8
KEY_W = 1024
VAL_W = 1024
QKV_W = 3072
CONV_W = 4
D_FF = 4096
PAST_LEN = 16384
EPS = 1e-6

LANES = 128
SUBLANES = 8
SLAB = 128
IN_PROJ_TM = 512
OUT_FFN_TM = 256
SMALL_TM = 384
PROJ_CHUNK = 256
GB_W = 128
VMEM_LIMIT = 56 * 1024 * 1024

F32 = jnp.float32
BF16 = jnp.bfloat16
NEG_BIG = -1e30
NEG_LOG2_E = -1.4426950408889634


def _dot(a, b):
    return jnp.dot(a.astype(BF16), b.astype(BF16), preferred_element_type=F32)


def _dot_nt(a, b):
    return lax.dot_general(a.astype(BF16), b.astype(BF16), (((1,), (1,)), ((), ())),
                           preferred_element_type=F32)


def _dot_tn(a, b):
    return lax.dot_general(a.astype(BF16), b.astype(BF16), (((0,), (0,)), ((), ())),
                           preferred_element_type=F32)


def _bf16_pieces(x):
    hi = x.astype(BF16)
    rest = x - hi.astype(F32)
    mid = rest.astype(BF16)
    lo = (rest - mid.astype(F32)).astype(BF16)
    return hi, mid, lo


def _dot_exact_mask(mask, pieces):
    m = mask.astype(F32).astype(BF16)
    hi, mid, lo = (jnp.dot(m, p, preferred_element_type=F32) for p in pieces)
    return hi + mid + lo


def _rms(x, g):
    return x * lax.rsqrt(jnp.mean(x * x, axis=-1, keepdims=True) + EPS) * g


def _sigmoid(x):
    return 1.0 / (1.0 + jnp.exp2(x * NEG_LOG2_E))


def _silu(x):
    return x * _sigmoid(x)


def _softplus(x):
    return jnp.maximum(x, 0.0) + jnp.log1p(jnp.exp(-jnp.abs(x)))


def _l2_normalize(x, scale=1.0):
    return x * (lax.rsqrt(jnp.sum(x * x, axis=-1, keepdims=True) + EPS) * scale)


def _causal_conv_silu(ext_ref, seq, row0, n_rows, lanes, convw_ref):
    if seq is None and row0 >= SUBLANES and row0 % SUBLANES == 0 and n_rows % SUBLANES == 0:
        full = ext_ref[row0 - SUBLANES:row0 + n_rows, lanes]
        acc = full[SUBLANES:] * convw_ref[CONV_W - 1:CONV_W, lanes]
        for d in range(1, CONV_W):
            tap = pltpu.roll(full, d, axis=0)[SUBLANES:]
            acc = acc + tap * convw_ref[CONV_W - 1 - d:CONV_W - d, lanes]
        return _silu(acc)
    acc = None
    for j in range(CONV_W):
        lo = row0 - (CONV_W - 1) + j
        rows = ext_ref[lo:lo + n_rows, lanes] if seq is None else ext_ref[seq, lo:lo + n_rows, lanes]
        term = rows * convw_ref[j:j + 1, lanes]
        acc = term if acc is None else acc + term
    return _silu(acc)


def _beta_decay_cols(ba, pcol_ref):
    lane = lax.broadcasted_iota(jnp.int32, ba.shape, 1)
    g_col = -jnp.exp(pcol_ref[0:1, :]) * _softplus(ba + pcol_ref[1:2, :])
    return jnp.where(lane < N_HEADS, _sigmoid(ba), jnp.where(lane < 2 * N_HEADS, g_col, 0.0))


def _in_proj_kernel(x_ref, g_ref, wu_ref, wqkv_ref, wz_ref, wg_ref, wba_ref, pcol_ref,
                    u_ref, qkv_ref, z_ref, gates_ref, gbc_ref):
    h = _rms(x_ref[...], g_ref[...]).astype(BF16)
    u_ref[...] = jnp.dot(h, wu_ref[...], preferred_element_type=F32)
    qkv_ref[...] = jnp.dot(h, wqkv_ref[...], preferred_element_type=F32)
    z_ref[...] = jnp.dot(h, wz_ref[...], preferred_element_type=F32)
    gates_ref[...] = jnp.dot(h, wg_ref[...], preferred_element_type=F32)
    gbc_ref[...] = _beta_decay_cols(jnp.dot(h, wba_ref[...], preferred_element_type=F32), pcol_ref)


def _in_proj_conv_kernel(tm, n_tiles,
                         x_ref, g_ref, wu_ref, wqkv_ref, wz_ref, wg_ref, wba_ref, pcol_ref,
                         convw_ref, convp_ref,
                         u_ref, qn_ref, kn_ref, v_ref, z_ref, gates_ref, gbc_ref, convo_ref,
                         ext):
    t = pl.program_id(1)
    base = SUBLANES
    if n_tiles > 1:
        @pl.when(t > 0)
        def _():
            ext[base - (CONV_W - 1):base, :] = ext[base + tm - (CONV_W - 1):base + tm, :]

    @pl.when(t == 0)
    def _():
        ext[base - (CONV_W - 1):base, :] = convp_ref[0]

    h = _rms(x_ref[...], g_ref[...]).astype(BF16)
    u_ref[...] = jnp.dot(h, wu_ref[...], preferred_element_type=F32)
    n_groups = QKV_W // LANES
    chunks = ([(z_ref, wz_ref, c0) for c0 in range(0, VAL_W, PROJ_CHUNK)]
              + [(gates_ref, wg_ref, c0) for c0 in range(0, 2 * D_MODEL, PROJ_CHUNK)])
    assert n_groups % len(chunks) == 0
    every = n_groups // len(chunks)
    groups_per_chunk = PROJ_CHUNK // LANES
    for c in range(n_groups):
        if c % groups_per_chunk == 0:
            cols = slice(c * LANES, c * LANES + PROJ_CHUNK)
            ext[base:base + tm, cols] = jnp.dot(h, wqkv_ref[:, cols], preferred_element_type=F32)
            convo_ref[0, :, cols] = ext[base + tm - (CONV_W - 1):base + tm, cols]
        lanes = slice(c * LANES, (c + 1) * LANES)
        head_lanes = slice((c % N_HEADS) * LANES, (c % N_HEADS + 1) * LANES)
        y = _causal_conv_silu(ext, None, base, tm, lanes, convw_ref)
        if c < N_HEADS:
            qn_ref[:, head_lanes] = _l2_normalize(y, HEAD_K ** -0.5)
        elif c < 2 * N_HEADS:
            kn_ref[:, head_lanes] = _l2_normalize(y)
        else:
            v_ref[:, head_lanes] = y
        if c % every == 0:
            out_ref, w_ref, c0 = chunks[c // every]
            out_ref[:, c0:c0 + PROJ_CHUNK] = jnp.dot(h, w_ref[:, c0:c0 + PROJ_CHUNK],
                                                     preferred_element_type=F32)
    gbc_ref[...] = _beta_decay_cols(jnp.dot(h, wba_ref[...], preferred_element_type=F32), pcol_ref)


def _in_proj(x, w, tm, conv_prefix=None, n_seq=1):
    n = x.shape[0]
    fused = conv_prefix is not None
    assert n % (tm * n_seq) == 0
    n_tiles = n // (tm * n_seq)
    if fused:
        grid = (n_seq, n_tiles)
        const = lambda s, t: (0, 0)
        tok = lambda s, t: (s * n_tiles + t, 0)
    else:
        grid = (n_tiles,)
        const = lambda i: (0, 0)
        tok = lambda i: (i, 0)
    resident = pl.Buffered(1)
    in_specs = [
        pl.BlockSpec((tm, D_MODEL), tok),
        pl.BlockSpec((1, D_MODEL), const),
        pl.BlockSpec((D_MODEL, POOL_WIDTH), const, pipeline_mode=resident),
        pl.BlockSpec((D_MODEL, QKV_W), const, pipeline_mode=resident),
        pl.BlockSpec((D_MODEL, VAL_W), const, pipeline_mode=resident),
        pl.BlockSpec((D_MODEL, 2 * D_MODEL), const, pipeline_mode=resident),
        pl.BlockSpec((D_MODEL, GB_W), const, pipeline_mode=resident),
        pl.BlockSpec((SUBLANES, GB_W), const),
    ]
    args = [x, w["g_pre_mix"], w["wu"], w["wqkv"], w["wz"], w["wg"], w["wba"], w["pcol"]]
    tail_specs = [
        pl.BlockSpec((tm, 2 * D_MODEL), tok),
        pl.BlockSpec((tm, GB_W), tok),
    ]
    tail_shapes = [
        jax.ShapeDtypeStruct((n, 2 * D_MODEL), F32),
        jax.ShapeDtypeStruct((n, GB_W), F32),
    ]
    if fused:
        in_specs += [pl.BlockSpec((CONV_W, QKV_W), const),
                     pl.BlockSpec((1, CONV_W - 1, QKV_W), lambda s, t: (0, 0, 0))]
        args += [w["conv_w"], conv_prefix]
        out_specs = [
            pl.BlockSpec((tm, POOL_WIDTH), tok),
            pl.BlockSpec((tm, KEY_W), tok),
            pl.BlockSpec((tm, KEY_W), tok),
            pl.BlockSpec((tm, VAL_W), tok),
            pl.BlockSpec((tm, VAL_W), tok),
        ] + tail_specs + [pl.BlockSpec((1, CONV_W - 1, QKV_W), lambda s, t: (s, 0, 0))]
        out_shape = [
            jax.ShapeDtypeStruct((n, POOL_WIDTH), F32),
            jax.ShapeDtypeStruct((n, KEY_W), F32),
            jax.ShapeDtypeStruct((n, KEY_W), F32),
            jax.ShapeDtypeStruct((n, VAL_W), F32),
            jax.ShapeDtypeStruct((n, VAL_W), F32),
        ] + tail_shapes + [jax.ShapeDtypeStruct((n_seq, CONV_W - 1, QKV_W), F32)]
        body = functools.partial(_in_proj_conv_kernel, tm, n_tiles)
        scratch = [pltpu.VMEM((SUBLANES + tm, QKV_W), F32)]
        semantics = ("parallel", "arbitrary")
    else:
        out_specs = [
            pl.BlockSpec((tm, POOL_WIDTH), tok),
            pl.BlockSpec((tm, QKV_W), tok),
            pl.BlockSpec((tm, VAL_W), tok),
        ] + tail_specs
        out_shape = [
            jax.ShapeDtypeStruct((n, POOL_WIDTH), F32),
            jax.ShapeDtypeStruct((n, QKV_W), F32),
            jax.ShapeDtypeStruct((n, VAL_W), F32),
        ] + tail_shapes
        body = _in_proj_kernel
        scratch = []
        semantics = ("parallel",)
    return pl.pallas_call(
        body,
        grid=grid,
        in_specs=in_specs,
        out_specs=out_specs,
        out_shape=out_shape,
        scratch_shapes=scratch,
        compiler_params=pltpu.CompilerParams(
            dimension_semantics=semantics, vmem_limit_bytes=VMEM_LIMIT),
        name="in_proj_conv" if fused else "in_proj",
    )(*args)


def _mixer_kernel(nb, nseq, seq_len, n_valid, pos0, n_steps, preconv, shared_state, n_passthrough,
                  *refs):
    refs = list(refs)
    rest = refs[10 + n_passthrough:]
    if preconv:
        (u_ref, qn_ref, kn_ref, v_ref, gbc_ref, poolp_ref, s0_ref,
         onorm_ref, poolw_ref, pscale_ref) = refs[:10]
        a_ref, d_ref, poolo_ref, so_ref, u_ext = rest
    else:
        (u_ref, qkv_ref, gbc_ref, convp_ref, poolp_ref, s0_ref,
         convw_ref, onorm_ref, poolw_ref, pscale_ref) = refs[:10]
        a_ref, d_ref, convo_ref, poolo_ref, so_ref, qkv_ext, u_ext = rest
    t = pl.program_id(1)
    conv_base = SUBLANES
    pool_base = 2 * SUBLANES
    last = n_valid if n_valid < seq_len else seq_len
    seqs = [(b, s) for b in range(nb) for s in range(nseq)]
    sid = lambda b, s: b * nseq + s
    src = lambda b, s: 0 if shared_state else sid(b, s)

    if n_steps > 1:
        @pl.when(t > 0)
        def _():
            for b, s in seqs:
                i = sid(b, s)
                u_ext[i, pool_base - POOL_STATE:pool_base, :] = (
                    u_ext[i, pool_base + seq_len - POOL_STATE:pool_base + seq_len, :])
                if not preconv:
                    qkv_ext[i, conv_base - (CONV_W - 1):conv_base, :] = (
                        qkv_ext[i, conv_base + seq_len - (CONV_W - 1):conv_base + seq_len, :])

    @pl.when(t == 0)
    def _():
        for b, s in seqs:
            i = sid(b, s)
            u_ext[i, pool_base - POOL_STATE:pool_base, :] = poolp_ref[src(b, s)]
            if not preconv:
                qkv_ext[i, conv_base - (CONV_W - 1):conv_base, :] = convp_ref[src(b, s)]
            if n_steps > 1:
                so_ref[i] = s0_ref[src(b, s)]

    def state_in(b, s, h):
        return so_ref[sid(b, s), h] if n_steps > 1 else s0_ref[src(b, s), h]

    for b, s in seqs:
        i = sid(b, s)
        rows = slice(s * seq_len, (s + 1) * seq_len)
        u_ext[i, pool_base:pool_base + seq_len, :] = u_ref[b, rows, :]
        poolo_ref[i] = u_ext[i, pool_base + last - POOL_STATE:pool_base + last, :]
        if not preconv:
            qkv_ext[i, conv_base:conv_base + seq_len, :] = qkv_ref[b, rows, :]
            convo_ref[i] = qkv_ext[i, conv_base + last - (CONV_W - 1):conv_base + last, :]

    for b in range(nb):
        for gi, win in enumerate(POOL_WINDOWS):
            lanes = slice(gi * POOL_GROUP_W, (gi + 1) * POOL_GROUP_W)
            if pos0 >= POOL_STATE:
                cnt = float(win)
            else:
                pos = lax.broadcasted_iota(jnp.int32, (seq_len, 1), 0) + (pos0 + 1) + t * seq_len
                cnt = jnp.minimum(pos, win).astype(F32)
            pieces = []
            for s in range(nseq):
                i = sid(b, s)
                tok = u_ext[i, pool_base:pool_base + seq_len, lanes]
                tot = tok
                for j in range(1, win):
                    tot = tot + u_ext[i, pool_base - j:pool_base - j + seq_len, lanes]
                pieces.append(tot / cnt - tok)
            dev = pieces[0] if nseq == 1 else jnp.concatenate(pieces, axis=0)
            a_ref[b, :, lanes] = _dot(dev, poolw_ref[gi]) * pscale_ref[:, lanes]

    ri = lax.broadcasted_iota(jnp.int32, (SLAB, SLAB), 0)
    ci = lax.broadcasted_iota(jnp.int32, (SLAB, SLAB), 1)
    shift = seq_len.bit_length() - 1
    same = (ri >> shift) == (ci >> shift)
    m_le = same & (ci <= ri)
    m_lt = same & (ci < ri)
    gbc, gc_col, gc_row, gl_col = [], [], [], []
    for b in range(nb):
        gbc_b = gbc_ref[b]
        if n_valid < seq_len:
            gbc_b = jnp.where(lax.broadcasted_iota(jnp.int32, gbc_b.shape, 0) < n_valid, gbc_b, 0.0)
        gbc.append(gbc_b)
        pieces = _bf16_pieces(gbc_b)
        gc_b = _dot_exact_mask(m_le, pieces)
        gc_col.append(gc_b)
        gc_row.append(gc_b.T)
        if nseq == 1:
            gl_col.append(gc_b[SLAB - 1:SLAB, :])
        else:
            gl_col.append(_dot_exact_mask(same, pieces))

    units = [(b, h) for b in range(nb) for h in range(N_HEADS)]
    heads = range(len(units))
    eye = (ri == ci).astype(F32)
    narrow = (lambda a: a.astype(BF16)) if seq_len % (2 * SUBLANES) == 0 else (lambda a: a)

    def conv(b, lanes):
        pieces = [_causal_conv_silu(qkv_ext, sid(b, s), conv_base, seq_len, lanes, convw_ref)
                  for s in range(nseq)]
        return pieces[0] if nseq == 1 else jnp.concatenate(pieces, axis=0)

    kq, k_op, v, beta, e_g, e_dec, carry, decay_le, decay_lb = [], [], [], [], [], [], [], [], []
    for b, h in units:
        hl = slice(h * HEAD_K, (h + 1) * HEAD_K)
        if preconv:
            q, k = narrow(qn_ref[b, :, hl]), narrow(kn_ref[b, :, hl])
            v.append(v_ref[b, :, hl])
        else:
            q = narrow(_l2_normalize(conv(b, hl), HEAD_K ** -0.5))
            k = narrow(_l2_normalize(conv(b, slice(KEY_W + h * HEAD_K, KEY_W + (h + 1) * HEAD_K))))
            v.append(conv(b, slice(2 * KEY_W + h * HEAD_V, 2 * KEY_W + (h + 1) * HEAD_V)))
        k_op.append(k)
        kq.append(jnp.concatenate([k, q], axis=0))

    kk_qk = [_dot_nt(kq[h], k_op[h]) for h in heads]

    k_s, q_s = [], []
    for u, (b, h) in enumerate(units):
        if nseq == 1:
            prod = _dot(kq[u], state_in(b, 0, h))
            k_s.append(prod[:SLAB])
            q_s.append(prod[SLAB:])
        else:
            ks_parts, qs_parts = [], []
            for s in range(nseq):
                rows = slice(s * seq_len, (s + 1) * seq_len)
                pair_rows = jnp.concatenate(
                    [kq[u][rows], kq[u][SLAB + s * seq_len:SLAB + (s + 1) * seq_len]], axis=0)
                prod = _dot(pair_rows, state_in(b, s, h))
                ks_parts.append(prod[:seq_len])
                qs_parts.append(prod[seq_len:])
            k_s.append(jnp.concatenate(ks_parts, axis=0))
            q_s.append(jnp.concatenate(qs_parts, axis=0))

    for b, h in units:
        bt = gbc[b][:, h:h + 1]
        g_c = gc_col[b][:, N_HEADS + h:N_HEADS + h + 1]
        g_r = gc_row[b][N_HEADS + h:N_HEADS + h + 1, :]
        g_l = gl_col[b][:, N_HEADS + h:N_HEADS + h + 1]
        d_le = jnp.exp(jnp.where(m_le, g_c - g_r, NEG_BIG))
        decay_le.append(d_le)
        decay_lb.append(jnp.where(m_lt, d_le, 0.0) * bt)
        beta.append(bt)
        e_g.append(jnp.exp(g_c))
        e_dec.append(jnp.exp(g_l - g_c))
        carry.append([jnp.exp(g_l[s * seq_len:s * seq_len + 1, :]) for s in range(nseq)])

    lmat = [kk_qk[h][:SLAB] * decay_lb[h] for h in heads]
    attn = [(kk_qk[h][SLAB:] * decay_le[h]).astype(BF16) for h in heads]
    rhs =[(beta[h] * (v[h] - e_g[h] * k_s[h])).astype(BF16) for h in heads]

    pair = (ri >> 1) == (ci >> 1)
    l_bf = [lmat[h].astype(BF16) for h in heads]
    inv = [eye - jnp.where(pair, lmat[h], 0.0) for h in heads]
    blk = 2
    while blk < seq_len:
        sh = blk.bit_length() - 1
        lower_left = ((ri >> (sh + 1)) == (ci >> (sh + 1))) & ((ri >> sh) != (ci >> sh))
        inv_bf = [inv[h].astype(BF16) for h in heads]
        half = [_dot(inv_bf[h], l_bf[h]).astype(BF16) for h in heads]
        inv = [inv[h] - jnp.where(lower_left, _dot(half[h], inv_bf[h]), 0.0) for h in heads]
        blk *= 2

    v_new = [_dot(inv[h], rhs[h]) for h in heads]
    v_att = [narrow(v_new[h]) for h in heads]
    v_dec = [narrow(v_new[h] * e_dec[h]) for h in heads]
    for u, (b, h) in enumerate(units):
        for s in range(nseq):
            rows = slice(s * seq_len, (s + 1) * seq_len)
            so_ref[sid(b, s), h] = (state_in(b, s, h) * carry[u][s]
                                    + _dot_tn(k_op[u][rows], v_dec[u][rows]))
    for u, (b, h) in enumerate(units):
        o = e_g[u] * q_s[u] + _dot(attn[u], v_att[u])
        d_ref[b, :, h * HEAD_V:(h + 1) * HEAD_V] = (
            o * lax.rsqrt(jnp.mean(o * o, axis=-1, keepdims=True) + EPS) * onorm_ref[...])


def _mixer(tokens, gbc, convp, poolp, s0, w, *, nb, nseq, seq_len, n_valid, pos0,
           n_groups, n_steps, row0, shared_state, layer=None, prev_ssm=None,
           extra_out_rows=0, prev_ad=None):
    assert nseq * seq_len == SLAB and (nseq == 1 or n_steps == 1)
    assert layer is None or not shared_state
    preconv = convp is None
    n_tok = gbc.shape[0]
    row_len = n_steps * SLAB
    n_rows_in = n_tok // row_len
    seq_blk = nb * nseq
    n_seq_total = n_groups * seq_blk
    tok = lambda gi, t: (row0 + gi, t, 0)
    c2 = lambda gi, t: (0, 0)
    c3 = lambda gi, t: (0, 0, 0)

    def state_specs(per_seq, stacked_like=None):
        zeros = (0,) * len(per_seq)
        blk = (seq_blk,) + per_seq
        if stacked_like is not None:
            spec = pl.BlockSpec((None,) + blk, lambda gi, t: (layer, gi) + zeros)
            return spec, spec, jax.ShapeDtypeStruct(stacked_like.shape, F32)
        in_spec = (pl.BlockSpec((1,) + per_seq, lambda gi, t: (0,) + zeros) if shared_state
                   else pl.BlockSpec(blk, lambda gi, t: (gi,) + zeros))
        out_spec = pl.BlockSpec(blk, lambda gi, t: (gi,) + zeros)
        return in_spec, out_spec, jax.ShapeDtypeStruct((n_seq_total,) + per_seq, F32)

    pool_in, pool_out, pool_shape = state_specs((POOL_STATE, POOL_WIDTH))
    s0_spec, so_spec, so_shape = state_specs((N_HEADS, HEAD_K, HEAD_V),
                                             s0 if layer is not None else None)
    widths = [a.shape[1] for a in tokens]
    in_specs = [pl.BlockSpec((nb, SLAB, wd), tok) for wd in widths]
    in_specs.append(pl.BlockSpec((nb, SLAB, GB_W), tok))
    args = [a.reshape(n_rows_in, row_len, a.shape[1]) for a in tokens]
    args.append(gbc.reshape(n_rows_in, row_len, GB_W))
    if not preconv:
        conv_in, conv_out, conv_shape = state_specs((CONV_W - 1, QKV_W))
        in_specs.append(conv_in)
        args.append(convp)
    in_specs += [pool_in, s0_spec]
    args += [poolp, s0]
    if not preconv:
        in_specs.append(pl.BlockSpec((CONV_W, QKV_W), c2))
        args.append(w["conv_w"])
    in_specs += [pl.BlockSpec((1, HEAD_V), c2),
                 pl.BlockSpec((POOL_GROUPS, POOL_GROUP_W, POOL_GROUP_W), c3),
                 pl.BlockSpec((1, POOL_WIDTH), c2)]
    args += [w["o_norm_g"], w["pool_w"], w["pool_scale"]]
    out_row0 = row0 if prev_ad is not None else 0
    n_rows_out = n_rows_in if prev_ad is not None else n_groups * nb + extra_out_rows
    out_tok = lambda gi, t: (out_row0 + gi, t, 0)
    out_specs = [pl.BlockSpec((nb, SLAB, POOL_WIDTH), out_tok),
                 pl.BlockSpec((nb, SLAB, VAL_W), out_tok)]
    out_shape = [jax.ShapeDtypeStruct((n_rows_out, row_len, POOL_WIDTH), F32),
                 jax.ShapeDtypeStruct((n_rows_out, row_len, VAL_W), F32)]
    scratch = []
    if not preconv:
        out_specs.append(conv_out)
        out_shape.append(conv_shape)
        scratch.append(pltpu.VMEM((seq_blk, SUBLANES + seq_len, QKV_W), F32))
    out_specs += [pool_out, so_spec]
    out_shape += [pool_shape, so_shape]
    scratch.append(pltpu.VMEM((seq_blk, 2 * SUBLANES + seq_len, POOL_WIDTH), F32))
    aliases = {}
    passthrough = []
    if prev_ad is not None:
        passthrough += [(p.reshape(n_rows_out, row_len, p.shape[1]), i)
                        for i, p in enumerate(prev_ad)]
    if prev_ssm is not None:
        passthrough.append((prev_ssm, len(out_shape) - 1))
    for arr, out_index in passthrough:
        in_specs.append(pl.BlockSpec(memory_space=pl.ANY))
        args.append(arr)
        aliases[len(args) - 1] = out_index
    outs = pl.pallas_call(
        functools.partial(_mixer_kernel, nb, nseq, seq_len, n_valid, pos0, n_steps, preconv,
                          shared_state, len(aliases)),
        grid=(n_groups, n_steps),
        in_specs=in_specs,
        out_specs=out_specs,
        out_shape=out_shape,
        scratch_shapes=scratch,
        input_output_aliases=aliases,
        compiler_params=pltpu.CompilerParams(
            dimension_semantics=("parallel", "arbitrary"), vmem_limit_bytes=VMEM_LIMIT),
        name="mixer_%dx%dx%d%s" % (nb, nseq, seq_len, "_preconv" if preconv else ""),
    )(*args)
    a_out, d_out = (o.reshape(n_rows_out * row_len, o.shape[2]) for o in outs[:2])
    return [a_out, d_out] + list(outs[2:])


def _out_ffn_kernel(x_ref, a_ref, d_ref, z_ref, gates_ref, wbp_ref, wbd_ref, wout_ref, wup_ref,
                    wdown_ref, gpm_ref, gpf_ref, gqf_ref, y_ref):
    br_pool = jnp.dot(a_ref[...].astype(BF16), wbp_ref[...], preferred_element_type=F32)
    delta = (d_ref[...] * _silu(z_ref[...])).astype(BF16)
    br_delta = jnp.dot(delta, wbd_ref[...], preferred_element_type=F32)
    gate_pool = _sigmoid(gates_ref[:, :D_MODEL])
    gate_delta = _sigmoid(gates_ref[:, D_MODEL:])
    merged = (gate_pool * br_pool + gate_delta * br_delta).astype(BF16)
    m = jnp.dot(merged, wout_ref[...], preferred_element_type=F32)
    x = x_ref[...] + _rms(m, gpm_ref[...])
    h2 = _rms(x, gpf_ref[...]).astype(BF16)
    up = jnp.dot(h2, wup_ref[...], preferred_element_type=F32)
    act = jnp.square(jnp.maximum(up, 0.0)).astype(BF16)
    f = jnp.dot(act, wdown_ref[...], preferred_element_type=F32)
    y_ref[...] = x + _rms(f, gqf_ref[...])


def _out_ffn(x, a, d, z, gates, w, tm, layer):
    n = x.shape[0]
    assert n % tm == 0
    const = lambda i: (0, 0)
    tok = lambda i: (i, 0)
    resident = pl.Buffered(1)

    def weight(rows, cols):
        return pl.BlockSpec((None, rows, cols), lambda i: (layer, 0, 0), pipeline_mode=resident)

    in_specs = [
        pl.BlockSpec((tm, D_MODEL), tok),
        pl.BlockSpec((tm, POOL_WIDTH), tok),
        pl.BlockSpec((tm, VAL_W), tok),
        pl.BlockSpec((tm, VAL_W), tok),
        pl.BlockSpec((tm, 2 * D_MODEL), tok),
        weight(POOL_WIDTH, D_MODEL),
        weight(VAL_W, D_MODEL),
        weight(D_MODEL, D_MODEL),
        weight(D_MODEL, D_FF),
        weight(D_FF, D_MODEL),
        pl.BlockSpec((1, D_MODEL), const),
        pl.BlockSpec((1, D_MODEL), const),
        pl.BlockSpec((1, D_MODEL), const),
    ]
    return pl.pallas_call(
        _out_ffn_kernel,
        grid=(n // tm,),
        in_specs=in_specs,
        out_specs=pl.BlockSpec((tm, D_MODEL), tok),
        out_shape=jax.ShapeDtypeStruct((n, D_MODEL), F32),
        compiler_params=pltpu.CompilerParams(
            dimension_semantics=("parallel",), vmem_limit_bytes=VMEM_LIMIT),
        name="out_ffn",
    )(x, a, d, z, gates, w["wbp"], w["wbd"], w["wout"], w["wup"], w["wdown"],
      w["g_post_mix"], w["g_pre_ffn"], w["g_post_ffn"])


def _layer_weights(l, g_pre_mix, w_in, conv_w, a_log, dt_bias, o_norm_g, pool_w, pool_scale,
                   w_branch_pool, w_branch_delta, w_out, g_post_mix, g_pre_ffn, w_up, w_down,
                   g_post_ffn):
    wi = w_in[l]
    o_qkv = POOL_WIDTH
    o_b = o_qkv + QKV_W
    o_z = o_b + 2 * N_HEADS
    o_g = o_z + VAL_W
    wba = wi[:, o_b:o_z]
    zeros8 = jnp.zeros((N_HEADS,), F32)
    a_pad = jnp.concatenate([zeros8, a_log[l]])
    dt_pad = jnp.concatenate([zeros8, dt_bias[l]])
    params = jnp.stack([a_pad, dt_pad])
    pcol = jnp.pad(params, ((0, SUBLANES - 2), (0, GB_W - 2 * N_HEADS)))
    return {
        "g_pre_mix": g_pre_mix[l][None],
        "wu": wi[:, :o_qkv].astype(BF16),
        "wqkv": wi[:, o_qkv:o_b].astype(BF16),
        "wz": wi[:, o_z:o_g].astype(BF16),
        "wg": wi[:, o_g:].astype(BF16),
        "wba": jnp.pad(wba, ((0, 0), (0, GB_W - 2 * N_HEADS))).astype(BF16),
        "pcol": pcol,
        "conv_w": conv_w[l],
        "o_norm_g": o_norm_g[l][None],
        "pool_w": pool_w[l].astype(BF16),
        "pool_scale": pool_scale[l][None],
        "wbp": w_branch_pool.astype(BF16),
        "wbd": w_branch_delta.astype(BF16),
        "wout": w_out.astype(BF16),
        "wup": w_up.astype(BF16),
        "wdown": w_down.astype(BF16),
        "g_post_mix": g_post_mix[l][None],
        "g_pre_ffn": g_pre_ffn[l][None],
        "g_post_ffn": g_post_ffn[l][None],
    }


def kernel(x_prompt, x_sample, state_conv, state_ssm, state_pool, meta_tokens, g_pre_mix, w_in, conv_w, a_log, dt_bias, o_norm_g, pool_w, pool_scale, w_branch_pool, w_branch_delta, w_out, g_post_mix, g_pre_ffn, w_up, w_down, g_post_ffn):
    bp, seq, _ = x_prompt.shape
    bs, dec_seq, _ = x_sample.shape
    depth = w_in.shape[0]
    n_prompt = bp * seq
    n_sample = bs * dec_seq
    assert seq % IN_PROJ_TM == 0 and n_prompt % OUT_FFN_TM == 0
    assert n_sample % SLAB == 0 and SLAB % dec_seq == 0 and N_META <= SLAB
    xp = x_prompt.reshape(n_prompt, D_MODEL)
    xs = jnp.concatenate([x_sample.reshape(n_sample, D_MODEL), meta_tokens.astype(F32),
                          jnp.zeros((SLAB - N_META, D_MODEL), F32)], axis=0)
    n_small = n_sample + SLAB
    small_tm = SMALL_TM if n_small % SMALL_TM == 0 else SLAB
    meta_block = n_sample // SLAB
    zero_conv = jnp.zeros((1, CONV_W - 1, QKV_W), F32)
    zero_pool = jnp.zeros((1, POOL_STATE, POOL_WIDTH), F32)
    zero_ssm = jnp.zeros((1, N_HEADS, HEAD_K, HEAD_V), F32)
    state_ssm = state_ssm.astype(F32)
    prompt_nb = 2 if bp % 2 == 0 else 1
    outs = [[] for _ in range(5)]
    ssm_s = None
    for l in range(depth):
        w = _layer_weights(l, g_pre_mix, w_in, conv_w, a_log, dt_bias, o_norm_g, pool_w, pool_scale,
                           w_branch_pool, w_branch_delta, w_out, g_post_mix, g_pre_ffn, w_up,
                           w_down, g_post_ffn)
        us, qkvs, zs, gatess, gbcs = _in_proj(xs, w, small_tm)
        a_s, d_s, conv_s, pool_s, ssm_s = _mixer(
            (us, qkvs), gbcs, state_conv[l], state_pool[l], state_ssm, w,
            nb=1, nseq=SLAB // dec_seq, seq_len=dec_seq, n_valid=dec_seq, pos0=PAST_LEN,
            n_groups=n_sample // SLAB, n_steps=1, row0=0, shared_state=False,
            layer=l, prev_ssm=ssm_s, extra_out_rows=1)
        a_small, d_small, conv_m, pool_m, ssm_m = _mixer(
            (us, qkvs), gbcs, zero_conv, zero_pool, zero_ssm, w,
            nb=1, nseq=1, seq_len=SLAB, n_valid=N_META, pos0=0, n_groups=1, n_steps=1,
            row0=meta_block, shared_state=True, prev_ad=(a_s, d_s))
        up, qnp, knp, vp, zp, gatesp, gbcp, conv_p = _in_proj(
            xp, w, IN_PROJ_TM, conv_prefix=conv_m, n_seq=bp)
        a_p, d_p, pool_p, ssm_p = _mixer(
            (up, qnp, knp, vp), gbcp, None, pool_m, ssm_m, w,
            nb=prompt_nb, nseq=1, seq_len=SLAB, n_valid=SLAB, pos0=N_META,
            n_groups=bp // prompt_nb, n_steps=seq // SLAB, row0=0, shared_state=True)
        xp = _out_ffn(xp, a_p, d_p, zp, gatesp, w, OUT_FFN_TM, l)
        xs = _out_ffn(xs, a_small, d_small, zs, gatess, w, small_tm, l)
        for acc, val in zip(outs, (conv_p, ssm_p, pool_p, conv_s, pool_s)):
            acc.append(val)
    y_prompt = xp.reshape(bp, seq, D_MODEL)
    y_sample = xs[:n_sample].reshape(bs, dec_seq, D_MODEL)
    conv_p, ssm_p, pool_p, conv_s, pool_s = (jnp.stack(o) for o in outs)
    return (y_prompt, y_sample, conv_p, ssm_p, pool_p, conv_s, ssm_s, pool_s)
```

```python
import functools

import jax
import jax.numpy as jnp
from jax import lax
from jax.experimental import pallas as pl
from jax.experimental.pallas import tpu as pltpu

D_MODEL = 1024
N_META = 16
POOL_GROUPS = 4
POOL_GROUP_W = 128
POOL_WIDTH = 512
POOL_WINDOWS = (2, 4, 8, 16)
POOL_STATE = 15
HEAD_K = 128
HEAD_V = 128
N_HEADS = 8
KEY_W = 1024
VAL_W = 1024
QKV_W = 3072
CONV_W = 4
D_FF = 4096
PAST_LEN = 16384
EPS = 1e-6

LANES = 128
SUBLANES = 8
SLAB = 128
PROJ_CHUNK = 256
GB_W = 128
VMEM_LIMIT = 56 * 1024 * 1024

F32 = jnp.float32
BF16 = jnp.bfloat16
NEG_BIG = -1e30
NEG_LOG2_E = -1.4426950408889634


def _dot(a, b):
    return jnp.dot(a.astype(BF16), b.astype(BF16), preferred_element_type=F32)


def _dot_nt(a, b):
    return lax.dot_general(a.astype(BF16), b.astype(BF16), (((1,), (1,)), ((), ())),
                           preferred_element_type=F32)


def _dot_tn(a, b):
    return lax.dot_general(a.astype(BF16), b.astype(BF16), (((0,), (0,)), ((), ())),
                           preferred_element_type=F32)


def _bf16_pieces(x):
    hi = x.astype(BF16)
    rest = x - hi.astype(F32)
    mid = rest.astype(BF16)
    lo = (rest - mid.astype(F32)).astype(BF16)
    return hi, mid, lo


def _dot_exact_mask(mask, pieces):
    m = mask.astype(F32).astype(BF16)
    hi, mid, lo = (jnp.dot(m, p, preferred_element_type=F32) for p in pieces)
    return hi + mid + lo


def _rms(x, g):
    return x * lax.rsqrt(jnp.mean(x * x, axis=-1, keepdims=True) + EPS) * g


def _sigmoid(x):
    return 1.0 / (1.0 + jnp.exp2(x * NEG_LOG2_E))


def _silu(x):
    return x * _sigmoid(x)


def _softplus(x):
    return jnp.maximum(x, 0.0) + jnp.log1p(jnp.exp(-jnp.abs(x)))


def _l2_normalize(x, scale=1.0):
    return x * (lax.rsqrt(jnp.sum(x * x, axis=-1, keepdims=True) + EPS) * scale)


def _causal_conv_silu(ext_ref, seq, row0, n_rows, lanes, convw_ref):
    if seq is None and row0 >= SUBLANES and row0 % SUBLANES == 0 and n_rows % SUBLANES == 0:
        full = ext_ref[row0 - SUBLANES:row0 + n_rows, lanes]
        acc = full[SUBLANES:] * convw_ref[CONV_W - 1:CONV_W, lanes]
        for d in range(1, CONV_W):
            tap = pltpu.roll(full, d, axis=0)[SUBLANES:]
            acc = acc + tap * convw_ref[CONV_W - 1 - d:CONV_W - d, lanes]
        return _silu(acc)
    acc = None
    for j in range(CONV_W):
        lo = row0 - (CONV_W - 1) + j
        rows = ext_ref[lo:lo + n_rows, lanes] if seq is None else ext_ref[seq, lo:lo + n_rows, lanes]
        term = rows * convw_ref[j:j + 1, lanes]
        acc = term if acc is None else acc + term
    return _silu(acc)


def _beta_decay_cols(ba, pcol_ref):
    lane = lax.broadcasted_iota(jnp.int32, ba.shape, 1)
    g_col = -jnp.exp(pcol_ref[0:1, :]) * _softplus(ba + pcol_ref[1:2, :])
    return jnp.where(lane < N_HEADS, _sigmoid(ba), jnp.where(lane < 2 * N_HEADS, g_col, 0.0))


def _in_proj_kernel(x_ref, g_ref, wu_ref, wqkv_ref, wz_ref, wg_ref, wba_ref, pcol_ref,
                    u_ref, qkv_ref, z_ref, gates_ref, gbc_ref):
    h = _rms(x_ref[...], g_ref[...]).astype(BF16)
    u_ref[...] = jnp.dot(h, wu_ref[...], preferred_element_type=F32)
    qkv_ref[...] = jnp.dot(h, wqkv_ref[...], preferred_element_type=F32)
    z_ref[...] = jnp.dot(h, wz_ref[...], preferred_element_type=F32)
    gates_ref[...] = jnp.dot(h, wg_ref[...], preferred_element_type=F32)
    gbc_ref[...] = _beta_decay_cols(jnp.dot(h, wba_ref[...], preferred_element_type=F32), pcol_ref)


def _in_proj_conv_kernel(tm, n_tiles,
                         x_ref, g_ref, wu_ref, wqkv_ref, wz_ref, wg_ref, wba_ref, pcol_ref,
                         convw_ref, convp_ref,
                         u_ref, qn_ref, kn_ref, v_ref, z_ref, gates_ref, gbc_ref, convo_ref,
                         ext):
    t = pl.program_id(1)
    base = SUBLANES
    if n_tiles > 1:
        @pl.when(t > 0)
        def _():
            ext[base - (CONV_W - 1):base, :] = ext[base + tm - (CONV_W - 1):base + tm, :]

    @pl.when(t == 0)
    def _():
        ext[base - (CONV_W - 1):base, :] = convp_ref[0]

    h = _rms(x_ref[...], g_ref[...]).astype(BF16)
    u_ref[...] = jnp.dot(h, wu_ref[...], preferred_element_type=F32)
    n_groups = QKV_W // LANES
    chunks = ([(z_ref, wz_ref, c0) for c0 in range(0, VAL_W, PROJ_CHUNK)]
              + [(gates_ref, wg_ref, c0) for c0 in range(0, 2 * D_MODEL, PROJ_CHUNK)])
    assert n_groups % len(chunks) == 0
    every = n_groups // len(chunks)
    groups_per_chunk = PROJ_CHUNK // LANES
    for c in range(n_groups):
        if c % groups_per_chunk == 0:
            cols = slice(c * LANES, c * LANES + PROJ_CHUNK)
            ext[base:base + tm, cols] = jnp.dot(h, wqkv_ref[:, cols], preferred_element_type=F32)
            convo_ref[0, :, cols] = ext[base + tm - (CONV_W - 1):base + tm, cols]
        lanes = slice(c * LANES, (c + 1) * LANES)
        head_lanes = slice((c % N_HEADS) * LANES, (c % N_HEADS + 1) * LANES)
        y = _causal_conv_silu(ext, None, base, tm, lanes, convw_ref)
        if c < N_HEADS:
            qn_ref[:, head_lanes] = _l2_normalize(y, HEAD_K ** -0.5)
        elif c < 2 * N_HEADS:
            kn_ref[:, head_lanes] = _l2_normalize(y)
        else:
            v_ref[:, head_lanes] = y
        if c % every == 0:
            out_ref, w_ref, c0 = chunks[c // every]
            out_ref[:, c0:c0 + PROJ_CHUNK] = jnp.dot(h, w_ref[:, c0:c0 + PROJ_CHUNK],
                                                     preferred_element_type=F32)
    gbc_ref[...] = _beta_decay_cols(jnp.dot(h, wba_ref[...], preferred_element_type=F32), pcol_ref)


def _in_proj(x, w, tm, conv_prefix=None, n_seq=1):
    n = x.shape[0]
    fused = conv_prefix is not None
    assert n % (tm * n_seq) == 0
    n_tiles = n // (tm * n_seq)
    if fused:
        grid = (n_seq, n_tiles)
        const = lambda s, t: (0, 0)
        tok = lambda s, t: (s * n_tiles + t, 0)
    else:
        grid = (n_tiles,)
        const = lambda i: (0, 0)
        tok = lambda i: (i, 0)
    resident = pl.Buffered(1)
    in_specs = [
        pl.BlockSpec((tm, D_MODEL), tok),
        pl.BlockSpec((1, D_MODEL), const),
        pl.BlockSpec((D_MODEL, POOL_WIDTH), const, pipeline_mode=resident),
        pl.BlockSpec((D_MODEL, QKV_W), const, pipeline_mode=resident),
        pl.BlockSpec((D_MODEL, VAL_W), const, pipeline_mode=resident),
        pl.BlockSpec((D_MODEL, 2 * D_MODEL), const, pipeline_mode=resident),
        pl.BlockSpec((D_MODEL, GB_W), const, pipeline_mode=resident),
        pl.BlockSpec((SUBLANES, GB_W), const),
    ]
    args = [x, w["g_pre_mix"], w["wu"], w["wqkv"], w["wz"], w["wg"], w["wba"], w["pcol"]]
    tail_specs = [
        pl.BlockSpec((tm, 2 * D_MODEL), tok),
        pl.BlockSpec((tm, GB_W), tok),
    ]
    tail_shapes = [
        jax.ShapeDtypeStruct((n, 2 * D_MODEL), F32),
        jax.ShapeDtypeStruct((n, GB_W), F32),
    ]
    if fused:
        in_specs += [pl.BlockSpec((CONV_W, QKV_W), const),
                     pl.BlockSpec((1, CONV_W - 1, QKV_W), lambda s, t: (0, 0, 0))]
        args += [w["conv_w"], conv_prefix]
        out_specs = [
            pl.BlockSpec((tm, POOL_WIDTH), tok),
            pl.BlockSpec((tm, KEY_W), tok),
            pl.BlockSpec((tm, KEY_W), tok),
            pl.BlockSpec((tm, VAL_W), tok),
            pl.BlockSpec((tm, VAL_W), tok),
        ] + tail_specs + [pl.BlockSpec((1, CONV_W - 1, QKV_W), lambda s, t: (s, 0, 0))]
        out_shape = [
            jax.ShapeDtypeStruct((n, POOL_WIDTH), F32),
            jax.ShapeDtypeStruct((n, KEY_W), F32),
            jax.ShapeDtypeStruct((n, KEY_W), F32),
            jax.ShapeDtypeStruct((n, VAL_W), F32),
            jax.ShapeDtypeStruct((n, VAL_W), F32),
        ] + tail_shapes + [jax.ShapeDtypeStruct((n_seq, CONV_W - 1, QKV_W), F32)]
        body = functools.partial(_in_proj_conv_kernel, tm, n_tiles)
        scratch = [pltpu.VMEM((SUBLANES + tm, QKV_W), F32)]
        semantics = ("parallel", "arbitrary")
    else:
        out_specs = [
            pl.BlockSpec((tm, POOL_WIDTH), tok),
            pl.BlockSpec((tm, QKV_W), tok),
            pl.BlockSpec((tm, VAL_W), tok),
        ] + tail_specs
        out_shape = [
            jax.ShapeDtypeStruct((n, POOL_WIDTH), F32),
            jax.ShapeDtypeStruct((n, QKV_W), F32),
            jax.ShapeDtypeStruct((n, VAL_W), F32),
        ] + tail_shapes
        body = _in_proj_kernel
        scratch = []
        semantics = ("parallel",)
    return pl.pallas_call(
        body,
        grid=grid,
        in_specs=in_specs,
        out_specs=out_specs,
        out_shape=out_shape,
        scratch_shapes=scratch,
        compiler_params=pltpu.CompilerParams(
            dimension_semantics=semantics, vmem_limit_bytes=VMEM_LIMIT),
        name="in_proj_conv" if fused else "in_proj",
    )(*args)


def _mixer_kernel(nb, nseq, seq_len, n_valid, pos0, n_steps, preconv, shared_state, n_passthrough,
                  *refs):
    refs = list(refs)
    rest = refs[10 + n_passthrough:]
    if preconv:
        (u_ref, qn_ref, kn_ref, v_ref, gbc_ref, poolp_ref, s0_ref,
         onorm_ref, poolw_ref, pscale_ref) = refs[:10]
        a_ref, d_ref, poolo_ref, so_ref, u_ext = rest
    else:
        (u_ref, qkv_ref, gbc_ref, convp_ref, poolp_ref, s0_ref,
         convw_ref, onorm_ref, poolw_ref, pscale_ref) = refs[:10]
        a_ref, d_ref, convo_ref, poolo_ref, so_ref, qkv_ext, u_ext = rest
    t = pl.program_id(1)
    conv_base = SUBLANES
    pool_base = 2 * SUBLANES
    last = n_valid if n_valid < seq_len else seq_len
    seqs = [(b, s) for b in range(nb) for s in range(nseq)]
    sid = lambda b, s: b * nseq + s
    src = lambda b, s: 0 if shared_state else sid(b, s)

    if n_steps > 1:
        @pl.when(t > 0)
        def _():
            for b, s in seqs:
                i = sid(b, s)
                u_ext[i, pool_base - POOL_STATE:pool_base, :] = (
                    u_ext[i, pool_base + seq_len - POOL_STATE:pool_base + seq_len, :])
                if not preconv:
                    qkv_ext[i, conv_base - (CONV_W - 1):conv_base, :] = (
                        qkv_ext[i, conv_base + seq_len - (CONV_W - 1):conv_base + seq_len, :])

    @pl.when(t == 0)
    def _():
        for b, s in seqs:
            i = sid(b, s)
            u_ext[i, pool_base - POOL_STATE:pool_base, :] = poolp_ref[src(b, s)]
            if not preconv:
                qkv_ext[i, conv_base - (CONV_W - 1):conv_base, :] = convp_ref[src(b, s)]
            if n_steps > 1:
                so_ref[i] = s0_ref[src(b, s)]

    def state_in(b, s, h):
        return so_ref[sid(b, s), h] if n_steps > 1 else s0_ref[src(b, s), h]

    for b, s in seqs:
        i = sid(b, s)
        rows = slice(s * seq_len, (s + 1) * seq_len)
        u_ext[i, pool_base:pool_base + seq_len, :] = u_ref[b, rows, :]
        poolo_ref[i] = u_ext[i, pool_base + last - POOL_STATE:pool_base + last, :]
        if not preconv:
            qkv_ext[i, conv_base:conv_base + seq_len, :] = qkv_ref[b, rows, :]
            convo_ref[i] = qkv_ext[i, conv_base + last - (CONV_W - 1):conv_base + last, :]

    for b in range(nb):
        for gi, win in enumerate(POOL_WINDOWS):
            lanes = slice(gi * POOL_GROUP_W, (gi + 1) * POOL_GROUP_W)
            if pos0 >= POOL_STATE:
                cnt = float(win)
            else:
                pos = lax.broadcasted_iota(jnp.int32, (seq_len, 1), 0) + (pos0 + 1) + t * seq_len
                cnt = jnp.minimum(pos, win).astype(F32)
            pieces = []
            for s in range(nseq):
                i = sid(b, s)
                tok = u_ext[i, pool_base:pool_base + seq_len, lanes]
                tot = tok
                for j in range(1, win):
                    tot = tot + u_ext[i, pool_base - j:pool_base - j + seq_len, lanes]
                pieces.append(tot / cnt - tok)
            dev = pieces[0] if nseq == 1 else jnp.concatenate(pieces, axis=0)
            a_ref[b, :, lanes] = _dot(dev, poolw_ref[gi]) * pscale_ref[:, lanes]

    ri = lax.broadcasted_iota(jnp.int32, (SLAB, SLAB), 0)
    ci = lax.broadcasted_iota(jnp.int32, (SLAB, SLAB), 1)
    shift = seq_len.bit_length() - 1
    same = (ri >> shift) == (ci >> shift)
    m_le = same & (ci <= ri)
    m_lt = same & (ci < ri)
    gbc, gc_col, gc_row, gl_col = [], [], [], []
    for b in range(nb):
        gbc_b = gbc_ref[b]
        if n_valid < seq_len:
            gbc_b = jnp.where(lax.broadcasted_iota(jnp.int32, gbc_b.shape, 0) < n_valid, gbc_b, 0.0)
        gbc.append(gbc_b)
        pieces = _bf16_pieces(gbc_b)
        gc_b = _dot_exact_mask(m_le, pieces)
        gc_col.append(gc_b)
        gc_row.append(gc_b.T)
        if nseq == 1:
            gl_col.append(gc_b[SLAB - 1:SLAB, :])
        else:
            gl_col.append(_dot_exact_mask(same, pieces))

    units = [(b, h) for b in range(nb) for h in range(N_HEADS)]
    heads = range(len(units))
    eye = (ri == ci).astype(F32)
    narrow = (lambda a: a.astype(BF16)) if seq_len % (2 * SUBLANES) == 0 else (lambda a: a)

    def conv(b, lanes):
        pieces = [_causal_conv_silu(qkv_ext, sid(b, s), conv_base, seq_len, lanes, convw_ref)
                  for s in range(nseq)]
        return pieces[0] if nseq == 1 else jnp.concatenate(pieces, axis=0)

    kq, k_op, v, beta, e_g, e_dec, carry, decay_le, decay_lb = [], [], [], [], [], [], [], [], []
    for b, h in units:
        hl = slice(h * HEAD_K, (h + 1) * HEAD_K)
        if preconv:
            q, k = narrow(qn_ref[b, :, hl]), narrow(kn_ref[b, :, hl])
            v.append(v_ref[b, :, hl])
        else:
            q = narrow(_l2_normalize(conv(b, hl), HEAD_K ** -0.5))
            k = narrow(_l2_normalize(conv(b, slice(KEY_W + h * HEAD_K, KEY_W + (h + 1) * HEAD_K))))
            v.append(conv(b, slice(2 * KEY_W + h * HEAD_V, 2 * KEY_W + (h + 1) * HEAD_V)))
        k_op.append(k)
        kq.append(jnp.concatenate([k, q], axis=0))

    kk_qk = [_dot_nt(kq[h], k_op[h]) for h in heads]

    k_s, q_s = [], []
    for u, (b, h) in enumerate(units):
        if nseq == 1:
            prod = _dot(kq[u], state_in(b, 0, h))
            k_s.append(prod[:SLAB])
            q_s.append(prod[SLAB:])
        else:
            ks_parts, qs_parts = [], []
            for s in range(nseq):
                rows = slice(s * seq_len, (s + 1) * seq_len)
                pair_rows = jnp.concatenate(
                    [kq[u][rows], kq[u][SLAB + s * seq_len:SLAB + (s + 1) * seq_len]], axis=0)
                prod = _dot(pair_rows, state_in(b, s, h))
                ks_parts.append(prod[:seq_len])
                qs_parts.append(prod[seq_len:])
            k_s.append(jnp.concatenate(ks_parts, axis=0))
            q_s.append(jnp.concatenate(qs_parts, axis=0))

    for b, h in units:
        bt = gbc[b][:, h:h + 1]
        g_c = gc_col[b][:, N_HEADS + h:N_HEADS + h + 1]
        g_r = gc_row[b][N_HEADS + h:N_HEADS + h + 1, :]
        g_l = gl_col[b][:, N_HEADS + h:N_HEADS + h + 1]
        d_le = jnp.exp(jnp.where(m_le, g_c - g_r, NEG_BIG))
        decay_le.append(d_le)
        decay_lb.append(jnp.where(m_lt, d_le, 0.0) * bt)
        beta.append(bt)
        e_g.append(jnp.exp(g_c))
        e_dec.append(jnp.exp(g_l - g_c))
        carry.append([jnp.exp(g_l[s * seq_len:s * seq_len + 1, :]) for s in range(nseq)])

    lmat = [kk_qk[h][:SLAB] * decay_lb[h] for h in heads]
    attn = [(kk_qk[h][SLAB:] * decay_le[h]).astype(BF16) for h in heads]
    rhs =[(beta[h] * (v[h] - e_g[h] * k_s[h])).astype(BF16) for h in heads]

    pair = (ri >> 1) == (ci >> 1)
    l_bf = [lmat[h].astype(BF16) for h in heads]
    inv = [eye - jnp.where(pair, lmat[h], 0.0) for h in heads]
    blk = 2
    while blk < seq_len:
        sh = blk.bit_length() - 1
        lower_left = ((ri >> (sh + 1)) == (ci >> (sh + 1))) & ((ri >> sh) != (ci >> sh))
        inv_bf = [inv[h].astype(BF16) for h in heads]
        half = [_dot(inv_bf[h], l_bf[h]).astype(BF16) for h in heads]
        inv = [inv[h] - jnp.where(lower_left, _dot(half[h], inv_bf[h]), 0.0) for h in heads]
        blk *= 2

    v_new = [_dot(inv[h], rhs[h]) for h in heads]
    v_att = [narrow(v_new[h]) for h in heads]
    v_dec = [narrow(v_new[h] * e_dec[h]) for h in heads]
    for u, (b, h) in enumerate(units):
        for s in range(nseq):
            rows = slice(s * seq_len, (s + 1) * seq_len)
            so_ref[sid(b, s), h] = (state_in(b, s, h) * carry[u][s]
                                    + _dot_tn(k_op[u][rows], v_dec[u][rows]))
    for u, (b, h) in enumerate(units):
        o = e_g[u] * q_s[u] + _dot(attn[u], v_att[u])
        d_ref[b, :, h * HEAD_V:(h + 1) * HEAD_V] = (
            o * lax.rsqrt(jnp.mean(o * o, axis=-1, keepdims=True) + EPS) * onorm_ref[...])


def _mixer(tokens, gbc, convp, poolp, s0, w, *, nb, nseq, seq_len, n_valid, pos0,
           n_groups, n_steps, row0, shared_state, layer=None, prev_ssm=None,
           prev_ad=None):
    assert nseq * seq_len == SLAB and (nseq == 1 or n_steps == 1)
    assert layer is None or not shared_state
    preconv = convp is None
    n_tok = gbc.shape[0]
    row_len = n_steps * SLAB
    n_rows_in = n_tok // row_len
    seq_blk = nb * nseq
    n_seq_total = n_groups * seq_blk
    tok = lambda gi, t: (row0 + gi, t, 0)
    c2 = lambda gi, t: (0, 0)
    c3 = lambda gi, t: (0, 0, 0)

    def state_specs(per_seq, stacked_like=None):
        zeros = (0,) * len(per_seq)
        blk = (seq_blk,) + per_seq
        if stacked_like is not None:
            spec = pl.BlockSpec((None,) + blk, lambda gi, t: (layer, gi) + zeros)
            return spec, spec, jax.ShapeDtypeStruct(stacked_like.shape, F32)
        in_spec = (pl.BlockSpec((1,) + per_seq, lambda gi, t: (0,) + zeros) if shared_state
                   else pl.BlockSpec(blk, lambda gi, t: (gi,) + zeros))
        out_spec = pl.BlockSpec(blk, lambda gi, t: (gi,) + zeros)
        return in_spec, out_spec, jax.ShapeDtypeStruct((n_seq_total,) + per_seq, F32)

    pool_in, pool_out, pool_shape = state_specs((POOL_STATE, POOL_WIDTH))
    s0_spec, so_spec, so_shape = state_specs((N_HEADS, HEAD_K, HEAD_V),
                                             s0 if layer is not None else None)
    widths = [a.shape[1] for a in tokens]
    in_specs = [pl.BlockSpec((nb, SLAB, wd), tok) for wd in widths]
    in_specs.append(pl.BlockSpec((nb, SLAB, GB_W), tok))
    args = [a.reshape(n_rows_in, row_len, a.shape[1]) for a in tokens]
    args.append(gbc.reshape(n_rows_in, row_len, GB_W))
    if not preconv:
        conv_in, conv_out, conv_shape = state_specs((CONV_W - 1, QKV_W))
        in_specs.append(conv_in)
        args.append(convp)
    in_specs += [pool_in, s0_spec]
    args += [poolp, s0]
    if not preconv:
        in_specs.append(pl.BlockSpec((CONV_W, QKV_W), c2))
        args.append(w["conv_w"])
    in_specs += [pl.BlockSpec((1, HEAD_V), c2),
                 pl.BlockSpec((POOL_GROUPS, POOL_GROUP_W, POOL_GROUP_W), c3),
                 pl.BlockSpec((1, POOL_WIDTH), c2)]
    args += [w["o_norm_g"], w["pool_w"], w["pool_scale"]]
    out_row0 = row0 if prev_ad is not None else 0
    n_rows_out = n_rows_in if prev_ad is not None else n_groups * nb
    out_tok = lambda gi, t: (out_row0 + gi, t, 0)
    out_specs = [pl.BlockSpec((nb, SLAB, POOL_WIDTH), out_tok),
                 pl.BlockSpec((nb, SLAB, VAL_W), out_tok)]
    out_shape = [jax.ShapeDtypeStruct((n_rows_out, row_len, POOL_WIDTH), F32),
                 jax.ShapeDtypeStruct((n_rows_out, row_len, VAL_W), F32)]
    scratch = []
    if not preconv:
        out_specs.append(conv_out)
        out_shape.append(conv_shape)
        scratch.append(pltpu.VMEM((seq_blk, SUBLANES + seq_len, QKV_W), F32))
    out_specs += [pool_out, so_spec]
    out_shape += [pool_shape, so_shape]
    scratch.append(pltpu.VMEM((seq_blk, 2 * SUBLANES + seq_len, POOL_WIDTH), F32))
    aliases = {}
    passthrough = []
    if prev_ad is not None:
        passthrough += [(p.reshape(n_rows_out, row_len, p.shape[1]), i)
                        for i, p in enumerate(prev_ad)]
    if prev_ssm is not None:
        passthrough.append((prev_ssm, len(out_shape) - 1))
    for arr, out_index in passthrough:
        in_specs.append(pl.BlockSpec(memory_space=pl.ANY))
        args.append(arr)
        aliases[len(args) - 1] = out_index
    outs = pl.pallas_call(
        functools.partial(_mixer_kernel, nb, nseq, seq_len, n_valid, pos0, n_steps, preconv,
                          shared_state, len(aliases)),
        grid=(n_groups, n_steps),
        in_specs=in_specs,
        out_specs=out_specs,
        out_shape=out_shape,
        scratch_shapes=scratch,
        input_output_aliases=aliases,
        compiler_params=pltpu.CompilerParams(
            dimension_semantics=("parallel", "arbitrary"), vmem_limit_bytes=VMEM_LIMIT),
        name="mixer_%dx%dx%d%s" % (nb, nseq, seq_len, "_preconv" if preconv else ""),
    )(*args)
    a_out, d_out = (o.reshape(n_rows_out * row_len, o.shape[2]) for o in outs[:2])
    return [a_out, d_out] + list(outs[2:])


def _out_ffn_kernel(x_ref, a_ref, d_ref, z_ref, gates_ref, wbp_ref, wbd_ref, wout_ref, wup_ref,
                    wdown_ref, gpm_ref, gpf_ref, gqf_ref, y_ref):
    br_pool = jnp.dot(a_ref[...].astype(BF16), wbp_ref[...], preferred_element_type=F32)
    delta = (d_ref[...] * _silu(z_ref[...])).astype(BF16)
    br_delta = jnp.dot(delta, wbd_ref[...], preferred_element_type=F32)
    gate_pool = _sigmoid(gates_ref[:, :D_MODEL])
    gate_delta = _sigmoid(gates_ref[:, D_MODEL:])
    merged = (gate_pool * br_pool + gate_delta * br_delta).astype(BF16)
    m = jnp.dot(merged, wout_ref[...], preferred_element_type=F32)
    x = x_ref[...] + _rms(m, gpm_ref[...])
    h2 = _rms(x, gpf_ref[...]).astype(BF16)
    up = jnp.dot(h2, wup_ref[...], preferred_element_type=F32)
    act = jnp.square(jnp.maximum(up, 0.0)).astype(BF16)
    f = jnp.dot(act, wdown_ref[...], preferred_element_type=F32)
    y_ref[...] = x + _rms(f, gqf_ref[...])


def _out_ffn(x, a, d, z, gates, w, tm, layer):
    n = x.shape[0]
    assert n % tm == 0
    const = lambda i: (0, 0)
    tok = lambda i: (i, 0)
    resident = pl.Buffered(1)

    def weight(rows, cols):
        return pl.BlockSpec((None, rows, cols), lambda i: (layer, 0, 0), pipeline_mode=resident)

    in_specs = [
        pl.BlockSpec((tm, D_MODEL), tok),
        pl.BlockSpec((tm, POOL_WIDTH), tok),
        pl.BlockSpec((tm, VAL_W), tok),
        pl.BlockSpec((tm, VAL_W), tok),
        pl.BlockSpec((tm, 2 * D_MODEL), tok),
        weight(POOL_WIDTH, D_MODEL),
        weight(VAL_W, D_MODEL),
        weight(D_MODEL, D_MODEL),
        weight(D_MODEL, D_FF),
        weight(D_FF, D_MODEL),
        pl.BlockSpec((1, D_MODEL), const),
        pl.BlockSpec((1, D_MODEL), const),
        pl.BlockSpec((1, D_MODEL), const),
    ]
    return pl.pallas_call(
        _out_ffn_kernel,
        grid=(n // tm,),
        in_specs=in_specs,
        out_specs=pl.BlockSpec((tm, D_MODEL), tok),
        out_shape=jax.ShapeDtypeStruct((n, D_MODEL), F32),
        compiler_params=pltpu.CompilerParams(
            dimension_semantics=("parallel",), vmem_limit_bytes=VMEM_LIMIT),
        name="out_ffn",
    )(x, a, d, z, gates, w["wbp"], w["wbd"], w["wout"], w["wup"], w["wdown"],
      w["g_post_mix"], w["g_pre_ffn"], w["g_post_ffn"])


def _layer_weights(l, g_pre_mix, w_in, conv_w, a_log, dt_bias, o_norm_g, pool_w, pool_scale,
                   w_branch_pool, w_branch_delta, w_out, g_post_mix, g_pre_ffn, w_up, w_down,
                   g_post_ffn):
    wi = w_in[l]
    o_qkv = POOL_WIDTH
    o_b = o_qkv + QKV_W
    o_z = o_b + 2 * N_HEADS
    o_g = o_z + VAL_W
    wba = wi[:, o_b:o_z]
    zeros8 = jnp.zeros((N_HEADS,), F32)
    a_pad = jnp.concatenate([zeros8, a_log[l]])
    dt_pad = jnp.concatenate([zeros8, dt_bias[l]])
    params = jnp.stack([a_pad, dt_pad])
    pcol = jnp.pad(params, ((0, SUBLANES - 2), (0, GB_W - 2 * N_HEADS)))
    return {
        "g_pre_mix": g_pre_mix[l][None],
        "wu": wi[:, :o_qkv].astype(BF16),
        "wqkv": wi[:, o_qkv:o_b].astype(BF16),
        "wz": wi[:, o_z:o_g].astype(BF16),
        "wg": wi[:, o_g:].astype(BF16),
        "wba": jnp.pad(wba, ((0, 0), (0, GB_W - 2 * N_HEADS))).astype(BF16),
        "pcol": pcol,
        "conv_w": conv_w[l],
        "o_norm_g": o_norm_g[l][None],
        "pool_w": pool_w[l].astype(BF16),
        "pool_scale": pool_scale[l][None],
        "wbp": w_branch_pool.astype(BF16),
        "wbd": w_branch_delta.astype(BF16),
        "wout": w_out.astype(BF16),
        "wup": w_up.astype(BF16),
        "wdown": w_down.astype(BF16),
        "g_post_mix": g_post_mix[l][None],
        "g_pre_ffn": g_pre_ffn[l][None],
        "g_post_ffn": g_post_ffn[l][None],
    }


def kernel(x_prompt, x_sample, state_conv, state_ssm, state_pool, meta_tokens, g_pre_mix, w_in, conv_w, a_log, dt_bias, o_norm_g, pool_w, pool_scale, w_branch_pool, w_branch_delta, w_out, g_post_mix, g_pre_ffn, w_up, w_down, g_post_ffn):
    bp, seq, _ = x_prompt.shape
    bs, dec_seq, _ = x_sample.shape
    depth = w_in.shape[0]
    n_prompt = bp * seq
    n_sample = bs * dec_seq
    assert seq % 256 == 0 and n_sample % SLAB == 0 and SLAB % dec_seq == 0 and N_META <= SLAB
    xp = x_prompt.reshape(n_prompt, D_MODEL)
    xs = jnp.concatenate([x_sample.reshape(n_sample, D_MODEL), meta_tokens.astype(F32),
                          jnp.zeros((SLAB - N_META, D_MODEL), F32)], axis=0)
    n_small = n_sample + SLAB
    small_tm = 384 if n_small % 384 == 0 else SLAB
    meta_block = n_sample // SLAB
    zero_conv = jnp.zeros((1, CONV_W - 1, QKV_W), F32)
    zero_pool = jnp.zeros((1, POOL_STATE, POOL_WIDTH), F32)
    zero_ssm = jnp.zeros((1, N_HEADS, HEAD_K, HEAD_V), F32)
    state_ssm = state_ssm.astype(F32)
    prompt_nb = 2 if bp % 2 == 0 else 1
    outs = [[] for _ in range(5)]
    ssm_s = None
    for l in range(depth):
        w = _layer_weights(l, g_pre_mix, w_in, conv_w, a_log, dt_bias, o_norm_g, pool_w, pool_scale,
                           w_branch_pool, w_branch_delta, w_out, g_post_mix, g_pre_ffn, w_up,
                           w_down, g_post_ffn)
        us, qkvs, zs, gatess, gbcs = _in_proj(xs, w, small_tm)
        a_s, d_s, conv_s, pool_s, ssm_s = _mixer(
            (us, qkvs), gbcs, state_conv[l], state_pool[l], state_ssm, w,
            nb=1, nseq=SLAB // dec_seq, seq_len=dec_seq, n_valid=dec_seq, pos0=PAST_LEN,
            n_groups=n_sample // SLAB, n_steps=1, row0=0, shared_state=False,
            layer=l, prev_ssm=ssm_s,
            prev_ad=(jnp.zeros((n_small, POOL_WIDTH), F32), jnp.zeros((n_small, VAL_W), F32)))
        a_small, d_small, conv_m, pool_m, ssm_m = _mixer(
            (us, qkvs), gbcs, zero_conv, zero_pool, zero_ssm, w,
            nb=1, nseq=1, seq_len=SLAB, n_valid=N_META, pos0=0, n_groups=1, n_steps=1,
            row0=meta_block, shared_state=True, prev_ad=(a_s, d_s))
        up, qnp, knp, vp, zp, gatesp, gbcp, conv_p = _in_proj(
            xp, w, 256, conv_prefix=conv_m, n_seq=bp)
        a_p, d_p, pool_p, ssm_p = _mixer(
            (up, qnp, knp, vp), gbcp, None, pool_m, ssm_m, w,
            nb=prompt_nb, nseq=1, seq_len=SLAB, n_valid=SLAB, pos0=N_META,
            n_groups=bp // prompt_nb, n_steps=seq // SLAB, row0=0, shared_state=True)
        xp = _out_ffn(xp, a_p, d_p, zp, gatesp, w, 256, l)
        xs = _out_ffn(xs, a_small, d_small, zs, gatess, w, small_tm, l)
        for acc, val in zip(outs, (conv_p, ssm_p, pool_p, conv_s, pool_s)):
            acc.append(val)
    y_prompt = xp.reshape(bp, seq, D_MODEL)
    y_sample = xs[:n_sample].reshape(bs, dec_seq, D_MODEL)
    conv_p, ssm_p, pool_p, conv_s, pool_s = (jnp.stack(o) for o in outs)
    return (y_prompt, y_sample, conv_p, ssm_p, pool_p, conv_s, ssm_s, pool_s)
```

```python
import functools

import jax
import jax.numpy as jnp
from jax import lax
from jax.experimental import pallas as pl
from jax.experimental.pallas import tpu as pltpu

D_MODEL = 1024
N_META = 16
POOL_GROUPS = 4
POOL_GROUP_W = 128
POOL_WIDTH = 512
POOL_WINDOWS = (2, 4, 8, 16)
POOL_STATE = 15
HEAD_K = 128
HEAD_V = 128
N_HEADS = 8
KEY_W = 1024
VAL_W = 1024
QKV_W = 3072
CONV_W = 4
D_FF = 4096
PAST_LEN = 16384
EPS = 1e-6

LANES = 128
SUBLANES = 8
SLAB = 128
PROJ_CHUNK = 256
GB_W = 128
VMEM_LIMIT = 56 * 1024 * 1024

F32 = jnp.float32
BF16 = jnp.bfloat16
NEG_BIG = -1e30
NEG_LOG2_E = -1.4426950408889634


def _dot(a, b):
    return jnp.dot(a.astype(BF16), b.astype(BF16), preferred_element_type=F32)


def _dot_nt(a, b):
    return lax.dot_general(a.astype(BF16), b.astype(BF16), (((1,), (1,)), ((), ())),
                           preferred_element_type=F32)


def _dot_tn(a, b):
    return lax.dot_general(a.astype(BF16), b.astype(BF16), (((0,), (0,)), ((), ())),
                           preferred_element_type=F32)


def _bf16_pieces(x):
    hi = x.astype(BF16)
    rest = x - hi.astype(F32)
    mid = rest.astype(BF16)
    lo = (rest - mid.astype(F32)).astype(BF16)
    return hi, mid, lo


def _dot_exact_mask(mask, pieces):
    m = mask.astype(F32).astype(BF16)
    hi, mid, lo = (jnp.dot(m, p, preferred_element_type=F32) for p in pieces)
    return hi + mid + lo


def _rms(x, g):
    return x * lax.rsqrt(jnp.mean(x * x, axis=-1, keepdims=True) + EPS) * g


def _sigmoid(x):
    return 1.0 / (1.0 + jnp.exp2(x * NEG_LOG2_E))


def _silu(x):
    return x * _sigmoid(x)


def _softplus(x):
    return jnp.maximum(x, 0.0) + jnp.log1p(jnp.exp(-jnp.abs(x)))


def _l2_normalize(x, scale=1.0):
    return x * (lax.rsqrt(jnp.sum(x * x, axis=-1, keepdims=True) + EPS) * scale)


def _causal_conv_silu(ext_ref, seq, row0, n_rows, lanes, convw_ref):
    if seq is None and row0 >= SUBLANES and row0 % SUBLANES == 0 and n_rows % SUBLANES == 0:
        full = ext_ref[row0 - SUBLANES:row0 + n_rows, lanes]
        acc = full[SUBLANES:] * convw_ref[CONV_W - 1:CONV_W, lanes]
        for d in range(1, CONV_W):
            tap = pltpu.roll(full, d, axis=0)[SUBLANES:]
            acc = acc + tap * convw_ref[CONV_W - 1 - d:CONV_W - d, lanes]
        return _silu(acc)
    acc = None
    for j in range(CONV_W):
        lo = row0 - (CONV_W - 1) + j
        rows = ext_ref[lo:lo + n_rows, lanes] if seq is None else ext_ref[seq, lo:lo + n_rows, lanes]
        term = rows * convw_ref[j:j + 1, lanes]
        acc = term if acc is None else acc + term
    return _silu(acc)


def _beta_decay_cols(ba, pcol_ref):
    lane = lax.broadcasted_iota(jnp.int32, ba.shape, 1)
    g_col = -jnp.exp(pcol_ref[0:1, :]) * _softplus(ba + pcol_ref[1:2, :])
    return jnp.where(lane < N_HEADS, _sigmoid(ba), jnp.where(lane < 2 * N_HEADS, g_col, 0.0))


def _in_proj_kernel(x_ref, g_ref, wu_ref, wqkv_ref, wz_ref, wg_ref, wba_ref, pcol_ref,
                    u_ref, qkv_ref, z_ref, gates_ref, gbc_ref):
    h = _rms(x_ref[...], g_ref[...]).astype(BF16)
    u_ref[...] = jnp.dot(h, wu_ref[...], preferred_element_type=F32)
    qkv_ref[...] = jnp.dot(h, wqkv_ref[...], preferred_element_type=F32)
    z_ref[...] = jnp.dot(h, wz_ref[...], preferred_element_type=F32)
    gates_ref[...] = jnp.dot(h, wg_ref[...], preferred_element_type=F32)
    gbc_ref[...] = _beta_decay_cols(jnp.dot(h, wba_ref[...], preferred_element_type=F32), pcol_ref)


def _in_proj_conv_kernel(tm, n_tiles,
                         x_ref, g_ref, wu_ref, wqkv_ref, wz_ref, wg_ref, wba_ref, pcol_ref,
                         convw_ref, convp_ref,
                         u_ref, qn_ref, kn_ref, v_ref, z_ref, gates_ref, gbc_ref, convo_ref,
                         ext):
    t = pl.program_id(1)
    base = SUBLANES
    if n_tiles > 1:
        @pl.when(t > 0)
        def _():
            ext[base - (CONV_W - 1):base, :] = ext[base + tm - (CONV_W - 1):base + tm, :]

    @pl.when(t == 0)
    def _():
        ext[base - (CONV_W - 1):base, :] = convp_ref[0]

    h = _rms(x_ref[...], g_ref[...]).astype(BF16)
    u_ref[...] = jnp.dot(h, wu_ref[...], preferred_element_type=F32)
    n_groups = QKV_W // LANES
    chunks = ([(z_ref, wz_ref, c0) for c0 in range(0, VAL_W, PROJ_CHUNK)]
              + [(gates_ref, wg_ref, c0) for c0 in range(0, 2 * D_MODEL, PROJ_CHUNK)])
    assert n_groups % len(chunks) == 0
    every = n_groups // len(chunks)
    groups_per_chunk = PROJ_CHUNK // LANES
    for c in range(n_groups):
        if c % groups_per_chunk == 0:
            cols = slice(c * LANES, c * LANES + PROJ_CHUNK)
            ext[base:base + tm, cols] = jnp.dot(h, wqkv_ref[:, cols], preferred_element_type=F32)
            convo_ref[0, :, cols] = ext[base + tm - (CONV_W - 1):base + tm, cols]
        lanes = slice(c * LANES, (c + 1) * LANES)
        head_lanes = slice((c % N_HEADS) * LANES, (c % N_HEADS + 1) * LANES)
        y = _causal_conv_silu(ext, None, base, tm, lanes, convw_ref)
        if c < N_HEADS:
            qn_ref[:, head_lanes] = _l2_normalize(y, HEAD_K ** -0.5)
        elif c < 2 * N_HEADS:
            kn_ref[:, head_lanes] = _l2_normalize(y)
        else:
            v_ref[:, head_lanes] = y
        if c % every == 0:
            out_ref, w_ref, c0 = chunks[c // every]
            out_ref[:, c0:c0 + PROJ_CHUNK] = jnp.dot(h, w_ref[:, c0:c0 + PROJ_CHUNK],
                                                     preferred_element_type=F32)
    gbc_ref[...] = _beta_decay_cols(jnp.dot(h, wba_ref[...], preferred_element_type=F32), pcol_ref)


def _in_proj(x, w, tm, conv_prefix=None, n_seq=1):
    n = x.shape[0]
    fused = conv_prefix is not None
    assert n % (tm * n_seq) == 0
    n_tiles = n // (tm * n_seq)
    if fused:
        grid = (n_seq, n_tiles)
        const = lambda s, t: (0, 0)
        tok = lambda s, t: (s * n_tiles + t, 0)
    else:
        grid = (n_tiles,)
        const = lambda i: (0, 0)
        tok = lambda i: (i, 0)
    resident = pl.Buffered(1)
    in_specs = [
        pl.BlockSpec((tm, D_MODEL), tok),
        pl.BlockSpec((1, D_MODEL), const),
        pl.BlockSpec((D_MODEL, POOL_WIDTH), const, pipeline_mode=resident),
        pl.BlockSpec((D_MODEL, QKV_W), const, pipeline_mode=resident),
        pl.BlockSpec((D_MODEL, VAL_W), const, pipeline_mode=resident),
        pl.BlockSpec((D_MODEL, 2 * D_MODEL), const, pipeline_mode=resident),
        pl.BlockSpec((D_MODEL, GB_W), const, pipeline_mode=resident),
        pl.BlockSpec((SUBLANES, GB_W), const),
    ]
    args = [x, w["g_pre_mix"], w["wu"], w["wqkv"], w["wz"], w["wg"], w["wba"], w["pcol"]]
    tail_specs = [
        pl.BlockSpec((tm, 2 * D_MODEL), tok),
        pl.BlockSpec((tm, GB_W), tok),
    ]
    tail_shapes = [
        jax.ShapeDtypeStruct((n, 2 * D_MODEL), F32),
        jax.ShapeDtypeStruct((n, GB_W), F32),
    ]
    if fused:
        in_specs += [pl.BlockSpec((CONV_W, QKV_W), const),
                     pl.BlockSpec((1, CONV_W - 1, QKV_W), lambda s, t: (0, 0, 0))]
        args += [w["conv_w"], conv_prefix]
        out_specs = [
            pl.BlockSpec((tm, POOL_WIDTH), tok),
            pl.BlockSpec((tm, KEY_W), tok),
            pl.BlockSpec((tm, KEY_W), tok),
            pl.BlockSpec((tm, VAL_W), tok),
            pl.BlockSpec((tm, VAL_W), tok),
        ] + tail_specs + [pl.BlockSpec((1, CONV_W - 1, QKV_W), lambda s, t: (s, 0, 0))]
        out_shape = [
            jax.ShapeDtypeStruct((n, POOL_WIDTH), F32),
            jax.ShapeDtypeStruct((n, KEY_W), F32),
            jax.ShapeDtypeStruct((n, KEY_W), F32),
            jax.ShapeDtypeStruct((n, VAL_W), F32),
            jax.ShapeDtypeStruct((n, VAL_W), F32),
        ] + tail_shapes + [jax.ShapeDtypeStruct((n_seq, CONV_W - 1, QKV_W), F32)]
        body = functools.partial(_in_proj_conv_kernel, tm, n_tiles)
        scratch = [pltpu.VMEM((SUBLANES + tm, QKV_W), F32)]
        semantics = ("parallel", "arbitrary")
    else:
        out_specs = [
            pl.BlockSpec((tm, POOL_WIDTH), tok),
            pl.BlockSpec((tm, QKV_W), tok),
            pl.BlockSpec((tm, VAL_W), tok),
        ] + tail_specs
        out_shape = [
            jax.ShapeDtypeStruct((n, POOL_WIDTH), F32),
            jax.ShapeDtypeStruct((n, QKV_W), F32),
            jax.ShapeDtypeStruct((n, VAL_W), F32),
        ] + tail_shapes
        body = _in_proj_kernel
        scratch = []
        semantics = ("parallel",)
    return pl.pallas_call(
        body,
        grid=grid,
        in_specs=in_specs,
        out_specs=out_specs,
        out_shape=out_shape,
        scratch_shapes=scratch,
        compiler_params=pltpu.CompilerParams(
            dimension_semantics=semantics, vmem_limit_bytes=VMEM_LIMIT),
        name="in_proj_conv" if fused else "in_proj",
    )(*args)


def _mixer_kernel(nb, nseq, seq_len, n_valid, pos0, n_steps, preconv, shared_state, n_passthrough,
                  *refs):
    refs = list(refs)
    rest = refs[10 + n_passthrough:]
    if preconv:
        (u_ref, qn_ref, kn_ref, v_ref, gbc_ref, poolp_ref, s0_ref,
         onorm_ref, poolw_ref, pscale_ref) = refs[:10]
        a_ref, d_ref, poolo_ref, so_ref, u_ext = rest
    else:
        (u_ref, qkv_ref, gbc_ref, convp_ref, poolp_ref, s0_ref,
         convw_ref, onorm_ref, poolw_ref, pscale_ref) = refs[:10]
        a_ref, d_ref, convo_ref, poolo_ref, so_ref, qkv_ext, u_ext = rest
    t = pl.program_id(1)
    conv_base = SUBLANES
    pool_base = 2 * SUBLANES
    last = n_valid if n_valid < seq_len else seq_len
    seqs = [(b, s) for b in range(nb) for s in range(nseq)]
    sid = lambda b, s: b * nseq + s
    src = lambda b, s: 0 if shared_state else sid(b, s)

    if n_steps > 1:
        @pl.when(t > 0)
        def _():
            for b, s in seqs:
                i = sid(b, s)
                u_ext[i, pool_base - POOL_STATE:pool_base, :] = (
                    u_ext[i, pool_base + seq_len - POOL_STATE:pool_base + seq_len, :])
                if not preconv:
                    qkv_ext[i, conv_base - (CONV_W - 1):conv_base, :] = (
                        qkv_ext[i, conv_base + seq_len - (CONV_W - 1):conv_base + seq_len, :])

    @pl.when(t == 0)
    def _():
        for b, s in seqs:
            i = sid(b, s)
            u_ext[i, pool_base - POOL_STATE:pool_base, :] = poolp_ref[src(b, s)]
            if not preconv:
                qkv_ext[i, conv_base - (CONV_W - 1):conv_base, :] = convp_ref[src(b, s)]
            if n_steps > 1:
                so_ref[i] = s0_ref[src(b, s)]

    def state_in(b, s, h):
        return so_ref[sid(b, s), h] if n_steps > 1 else s0_ref[src(b, s), h]

    for b, s in seqs:
        i = sid(b, s)
        rows = slice(s * seq_len, (s + 1) * seq_len)
        u_ext[i, pool_base:pool_base + seq_len, :] = u_ref[b, rows, :]
        poolo_ref[i] = u_ext[i, pool_base + last - POOL_STATE:pool_base + last, :]
        if not preconv:
            qkv_ext[i, conv_base:conv_base + seq_len, :] = qkv_ref[b, rows, :]
            convo_ref[i] = qkv_ext[i, conv_base + last - (CONV_W - 1):conv_base + last, :]

    for b in range(nb):
        for gi, win in enumerate(POOL_WINDOWS):
            lanes = slice(gi * POOL_GROUP_W, (gi + 1) * POOL_GROUP_W)
            if pos0 >= POOL_STATE:
                cnt = float(win)
            else:
                pos = lax.broadcasted_iota(jnp.int32, (seq_len, 1), 0) + (pos0 + 1) + t * seq_len
                cnt = jnp.minimum(pos, win).astype(F32)
            pieces = []
            for s in range(nseq):
                i = sid(b, s)
                tok = u_ext[i, pool_base:pool_base + seq_len, lanes]
                tot = tok
                for j in range(1, win):
                    tot = tot + u_ext[i, pool_base - j:pool_base - j + seq_len, lanes]
                pieces.append(tot / cnt - tok)
            dev = pieces[0] if nseq == 1 else jnp.concatenate(pieces, axis=0)
            a_ref[b, :, lanes] = _dot(dev, poolw_ref[gi]) * pscale_ref[:, lanes]

    ri = lax.broadcasted_iota(jnp.int32, (SLAB, SLAB), 0)
    ci = lax.broadcasted_iota(jnp.int32, (SLAB, SLAB), 1)
    shift = seq_len.bit_length() - 1
    same = (ri >> shift) == (ci >> shift)
    m_le = same & (ci <= ri)
    m_lt = same & (ci < ri)
    gbc, gc_col, gc_row, gl_col = [], [], [], []
    for b in range(nb):
        gbc_b = gbc_ref[b]
        if n_valid < seq_len:
            gbc_b = jnp.where(lax.broadcasted_iota(jnp.int32, gbc_b.shape, 0) < n_valid, gbc_b, 0.0)
        gbc.append(gbc_b)
        pieces = _bf16_pieces(gbc_b)
        gc_b = _dot_exact_mask(m_le, pieces)
        gc_col.append(gc_b)
        gc_row.append(gc_b.T)
        if nseq == 1:
            gl_col.append(gc_b[SLAB - 1:SLAB, :])
        else:
            gl_col.append(_dot_exact_mask(same, pieces))

    units = [(b, h) for b in range(nb) for h in range(N_HEADS)]
    heads = range(len(units))
    eye = (ri == ci).astype(F32)
    narrow = (lambda a: a.astype(BF16)) if seq_len % (2 * SUBLANES) == 0 else (lambda a: a)

    def conv(b, lanes):
        pieces = [_causal_conv_silu(qkv_ext, sid(b, s), conv_base, seq_len, lanes, convw_ref)
                  for s in range(nseq)]
        return pieces[0] if nseq == 1 else jnp.concatenate(pieces, axis=0)

    kq, k_op, v, beta, e_g, e_dec, carry, decay_le, decay_lb = [], [], [], [], [], [], [], [], []
    for b, h in units:
        hl = slice(h * HEAD_K, (h + 1) * HEAD_K)
        if preconv:
            q, k = narrow(qn_ref[b, :, hl]), narrow(kn_ref[b, :, hl])
            v.append(v_ref[b, :, hl])
        else:
            q = narrow(_l2_normalize(conv(b, hl), HEAD_K ** -0.5))
            k = narrow(_l2_normalize(conv(b, slice(KEY_W + h * HEAD_K, KEY_W + (h + 1) * HEAD_K))))
            v.append(conv(b, slice(2 * KEY_W + h * HEAD_V, 2 * KEY_W + (h + 1) * HEAD_V)))
        k_op.append(k)
        kq.append(jnp.concatenate([k, q], axis=0))

    kk_qk = [_dot_nt(kq[h], k_op[h]) for h in heads]

    k_s, q_s = [], []
    for u, (b, h) in enumerate(units):
        if nseq == 1:
            prod = _dot(kq[u], state_in(b, 0, h))
            k_s.append(prod[:SLAB])
            q_s.append(prod[SLAB:])
        else:
            ks_parts, qs_parts = [], []
            for s in range(nseq):
                rows = slice(s * seq_len, (s + 1) * seq_len)
                pair_rows = jnp.concatenate(
                    [kq[u][rows], kq[u][SLAB + s * seq_len:SLAB + (s + 1) * seq_len]], axis=0)
                prod = _dot(pair_rows, state_in(b, s, h))
                ks_parts.append(prod[:seq_len])
                qs_parts.append(prod[seq_len:])
            k_s.append(jnp.concatenate(ks_parts, axis=0))
            q_s.append(jnp.concatenate(qs_parts, axis=0))

    for b, h in units:
        bt = gbc[b][:, h:h + 1]
        g_c = gc_col[b][:, N_HEADS + h:N_HEADS + h + 1]
        g_r = gc_row[b][N_HEADS + h:N_HEADS + h + 1, :]
        g_l = gl_col[b][:, N_HEADS + h:N_HEADS + h + 1]
        d_le = jnp.exp(jnp.where(m_le, g_c - g_r, NEG_BIG))
        decay_le.append(d_le)
        decay_lb.append(jnp.where(m_lt, d_le, 0.0) * bt)
        beta.append(bt)
        e_g.append(jnp.exp(g_c))
        e_dec.append(jnp.exp(g_l - g_c))
        carry.append([jnp.exp(g_l[s * seq_len:s * seq_len + 1, :]) for s in range(nseq)])

    lmat = [kk_qk[h][:SLAB] * decay_lb[h] for h in heads]
    attn = [(kk_qk[h][SLAB:] * decay_le[h]).astype(BF16) for h in heads]
    rhs =[(beta[h] * (v[h] - e_g[h] * k_s[h])).astype(BF16) for h in heads]

    pair = (ri >> 1) == (ci >> 1)
    l_bf = [lmat[h].astype(BF16) for h in heads]
    inv = [eye - jnp.where(pair, lmat[h], 0.0) for h in heads]
    blk = 2
    while blk < seq_len:
        sh = blk.bit_length() - 1
        lower_left = ((ri >> (sh + 1)) == (ci >> (sh + 1))) & ((ri >> sh) != (ci >> sh))
        inv_bf = [inv[h].astype(BF16) for h in heads]
        half = [_dot(inv_bf[h], l_bf[h]).astype(BF16) for h in heads]
        inv = [inv[h] - jnp.where(lower_left, _dot(half[h], inv_bf[h]), 0.0) for h in heads]
        blk *= 2

    v_new = [_dot(inv[h], rhs[h]) for h in heads]
    v_att = [narrow(v_new[h]) for h in heads]
    v_dec = [narrow(v_new[h] * e_dec[h]) for h in heads]
    for u, (b, h) in enumerate(units):
        for s in range(nseq):
            rows = slice(s * seq_len, (s + 1) * seq_len)
            so_ref[sid(b, s), h] = (state_in(b, s, h) * carry[u][s]
                                    + _dot_tn(k_op[u][rows], v_dec[u][rows]))
    for u, (b, h) in enumerate(units):
        o = e_g[u] * q_s[u] + _dot(attn[u], v_att[u])
        d_ref[b, :, h * HEAD_V:(h + 1) * HEAD_V] = (
            o * lax.rsqrt(jnp.mean(o * o, axis=-1, keepdims=True) + EPS) * onorm_ref[...])


def _mixer(tokens, gbc, convp, poolp, s0, w, *, nb, nseq, seq_len, n_valid, pos0,
           n_groups, n_steps, row0, shared_state, layer=None, prev_ssm=None,
           prev_ad=None):
    assert nseq * seq_len == SLAB and (nseq == 1 or n_steps == 1)
    assert layer is None or not shared_state
    preconv = convp is None
    n_tok = gbc.shape[0]
    row_len = n_steps * SLAB
    n_rows_in = n_tok // row_len
    seq_blk = nb * nseq
    n_seq_total = n_groups * seq_blk
    tok = lambda gi, t: (row0 + gi, t, 0)
    c2 = lambda gi, t: (0, 0)
    c3 = lambda gi, t: (0, 0, 0)

    def state_specs(per_seq, stacked_like=None):
        zeros = (0,) * len(per_seq)
        blk = (seq_blk,) + per_seq
        if stacked_like is not None:
            spec = pl.BlockSpec((None,) + blk, lambda gi, t: (layer, gi) + zeros)
            return spec, spec, jax.ShapeDtypeStruct(stacked_like.shape, F32)
        in_spec = (pl.BlockSpec((1,) + per_seq, lambda gi, t: (0,) + zeros) if shared_state
                   else pl.BlockSpec(blk, lambda gi, t: (gi,) + zeros))
        out_spec = pl.BlockSpec(blk, lambda gi, t: (gi,) + zeros)
        return in_spec, out_spec, jax.ShapeDtypeStruct((n_seq_total,) + per_seq, F32)

    pool_in, pool_out, pool_shape = state_specs((POOL_STATE, POOL_WIDTH))
    s0_spec, so_spec, so_shape = state_specs((N_HEADS, HEAD_K, HEAD_V),
                                             s0 if layer is not None else None)
    widths = [a.shape[1] for a in tokens]
    in_specs = [pl.BlockSpec((nb, SLAB, wd), tok) for wd in widths]
    in_specs.append(pl.BlockSpec((nb, SLAB, GB_W), tok))
    args = [a.reshape(n_rows_in, row_len, a.shape[1]) for a in tokens]
    args.append(gbc.reshape(n_rows_in, row_len, GB_W))
    if not preconv:
        conv_in, conv_out, conv_shape = state_specs((CONV_W - 1, QKV_W))
        in_specs.append(conv_in)
        args.append(convp)
    in_specs += [pool_in, s0_spec]
    args += [poolp, s0]
    if not preconv:
        in_specs.append(pl.BlockSpec((CONV_W, QKV_W), c2))
        args.append(w["conv_w"])
    in_specs += [pl.BlockSpec((1, HEAD_V), c2),
                 pl.BlockSpec((POOL_GROUPS, POOL_GROUP_W, POOL_GROUP_W), c3),
                 pl.BlockSpec((1, POOL_WIDTH), c2)]
    args += [w["o_norm_g"], w["pool_w"], w["pool_scale"]]
    out_row0 = row0 if prev_ad is not None else 0
    n_rows_out = n_rows_in if prev_ad is not None else n_groups * nb
    out_tok = lambda gi, t: (out_row0 + gi, t, 0)
    out_specs = [pl.BlockSpec((nb, SLAB, POOL_WIDTH), out_tok),
                 pl.BlockSpec((nb, SLAB, VAL_W), out_tok)]
    out_shape = [jax.ShapeDtypeStruct((n_rows_out, row_len, POOL_WIDTH), F32),
                 jax.ShapeDtypeStruct((n_rows_out, row_len, VAL_W), F32)]
    scratch = []
    if not preconv:
        out_specs.append(conv_out)
        out_shape.append(conv_shape)
        scratch.append(pltpu.VMEM((seq_blk, SUBLANES + seq_len, QKV_W), F32))
    out_specs += [pool_out, so_spec]
    out_shape += [pool_shape, so_shape]
    scratch.append(pltpu.VMEM((seq_blk, 2 * SUBLANES + seq_len, POOL_WIDTH), F32))
    aliases = {}
    passthrough = []
    if prev_ad is not None:
        passthrough += [(p.reshape(n_rows_out, row_len, p.shape[1]), i)
                        for i, p in enumerate(prev_ad)]
    if prev_ssm is not None:
        passthrough.append((prev_ssm, len(out_shape) - 1))
    for arr, out_index in passthrough:
        in_specs.append(pl.BlockSpec(memory_space=pl.ANY))
        args.append(arr)
        aliases[len(args) - 1] = out_index
    outs = pl.pallas_call(
        functools.partial(_mixer_kernel, nb, nseq, seq_len, n_valid, pos0, n_steps, preconv,
                          shared_state, len(aliases)),
        grid=(n_groups, n_steps),
        in_specs=in_specs,
        out_specs=out_specs,
        out_shape=out_shape,
        scratch_shapes=scratch,
        input_output_aliases=aliases,
        compiler_params=pltpu.CompilerParams(
            dimension_semantics=("parallel", "arbitrary"), vmem_limit_bytes=VMEM_LIMIT),
        name="mixer_%dx%dx%d%s" % (nb, nseq, seq_len, "_preconv" if preconv else ""),
    )(*args)
    a_out, d_out = (o.reshape(n_rows_out * row_len, o.shape[2]) for o in outs[:2])
    return [a_out, d_out] + list(outs[2:])


def _out_ffn_kernel(x_ref, a_ref, d_ref, z_ref, gates_ref, wbp_ref, wbd_ref, wout_ref, wup_ref,
                    wdown_ref, gpm_ref, gpf_ref, gqf_ref, y_ref):
    br_pool = jnp.dot(a_ref[...].astype(BF16), wbp_ref[...], preferred_element_type=F32)
    delta = (d_ref[...] * _silu(z_ref[...])).astype(BF16)
    br_delta = jnp.dot(delta, wbd_ref[...], preferred_element_type=F32)
    gate_pool = _sigmoid(gates_ref[:, :D_MODEL])
    gate_delta = _sigmoid(gates_ref[:, D_MODEL:])
    merged = (gate_pool * br_pool + gate_delta * br_delta).astype(BF16)
    m = jnp.dot(merged, wout_ref[...], preferred_element_type=F32)
    x = x_ref[...] + _rms(m, gpm_ref[...])
    h2 = _rms(x, gpf_ref[...]).astype(BF16)
    up = jnp.dot(h2, wup_ref[...], preferred_element_type=F32)
    act = jnp.square(jnp.maximum(up, 0.0)).astype(BF16)
    f = jnp.dot(act, wdown_ref[...], preferred_element_type=F32)
    y_ref[...] = x + _rms(f, gqf_ref[...])


def _out_ffn(x, a, d, z, gates, w, tm, layer):
    n = x.shape[0]
    assert n % tm == 0
    const = lambda i: (0, 0)
    tok = lambda i: (i, 0)
    resident = pl.Buffered(1)

    def weight(rows, cols):
        return pl.BlockSpec((None, rows, cols), lambda i: (layer, 0, 0), pipeline_mode=resident)

    in_specs = [
        pl.BlockSpec((tm, D_MODEL), tok),
        pl.BlockSpec((tm, POOL_WIDTH), tok),
        pl.BlockSpec((tm, VAL_W), tok),
        pl.BlockSpec((tm, VAL_W), tok),
        pl.BlockSpec((tm, 2 * D_MODEL), tok),
        weight(POOL_WIDTH, D_MODEL),
        weight(VAL_W, D_MODEL),
        weight(D_MODEL, D_MODEL),
        weight(D_MODEL, D_FF),
        weight(D_FF, D_MODEL),
        pl.BlockSpec((1, D_MODEL), const),
        pl.BlockSpec((1, D_MODEL), const),
        pl.BlockSpec((1, D_MODEL), const),
    ]
    return pl.pallas_call(
        _out_ffn_kernel,
        grid=(n // tm,),
        in_specs=in_specs,
        out_specs=pl.BlockSpec((tm, D_MODEL), tok),
        out_shape=jax.ShapeDtypeStruct((n, D_MODEL), F32),
        compiler_params=pltpu.CompilerParams(
            dimension_semantics=("parallel",), vmem_limit_bytes=VMEM_LIMIT),
        name="out_ffn",
    )(x, a, d, z, gates, w["wbp"], w["wbd"], w["wout"], w["wup"], w["wdown"],
      w["g_post_mix"], w["g_pre_ffn"], w["g_post_ffn"])


def _layer_weights(l, g_pre_mix, w_in, conv_w, a_log, dt_bias, o_norm_g, pool_w, pool_scale,
                   w_branch_pool, w_branch_delta, w_out, g_post_mix, g_pre_ffn, w_up, w_down,
                   g_post_ffn):
    wi = w_in[l]
    o_qkv = POOL_WIDTH
    o_b = o_qkv + QKV_W
    o_z = o_b + 2 * N_HEADS
    o_g = o_z + VAL_W
    wba = wi[:, o_b:o_z]
    zeros8 = jnp.zeros((N_HEADS,), F32)
    a_pad = jnp.concatenate([zeros8, a_log[l]])
    dt_pad = jnp.concatenate([zeros8, dt_bias[l]])
    params = jnp.stack([a_pad, dt_pad])
    pcol = jnp.pad(params, ((0, SUBLANES - 2), (0, GB_W - 2 * N_HEADS)))
    return {
        "g_pre_mix": g_pre_mix[l][None],
        "wu": wi[:, :o_qkv].astype(BF16),
        "wqkv": wi[:, o_qkv:o_b].astype(BF16),
        "wz": wi[:, o_z:o_g].astype(BF16),
        "wg": wi[:, o_g:].astype(BF16),
        "wba": jnp.pad(wba, ((0, 0), (0, GB_W - 2 * N_HEADS))).astype(BF16),
        "pcol": pcol,
        "conv_w": conv_w[l],
        "o_norm_g": o_norm_g[l][None],
        "pool_w": pool_w[l].astype(BF16),
        "pool_scale": pool_scale[l][None],
        "wbp": w_branch_pool.astype(BF16),
        "wbd": w_branch_delta.astype(BF16),
        "wout": w_out.astype(BF16),
        "wup": w_up.astype(BF16),
        "wdown": w_down.astype(BF16),
        "g_post_mix": g_post_mix[l][None],
        "g_pre_ffn": g_pre_ffn[l][None],
        "g_post_ffn": g_post_ffn[l][None],
    }


def kernel(x_prompt, x_sample, state_conv, state_ssm, state_pool, meta_tokens, g_pre_mix, w_in, conv_w, a_log, dt_bias, o_norm_g, pool_w, pool_scale, w_branch_pool, w_branch_delta, w_out, g_post_mix, g_pre_ffn, w_up, w_down, g_post_ffn):
    bp, seq, _ = x_prompt.shape
    bs, dec_seq, _ = x_sample.shape
    depth = w_in.shape[0]
    n_prompt = bp * seq
    n_sample = bs * dec_seq
    assert seq % 256 == 0 and n_sample % SLAB == 0 and SLAB % dec_seq == 0 and N_META <= SLAB
    xp = x_prompt.reshape(n_prompt, D_MODEL)
    xs = jnp.concatenate([x_sample.reshape(n_sample, D_MODEL), meta_tokens.astype(F32),
                          jnp.zeros((SLAB - N_META, D_MODEL), F32)], axis=0)
    n_small = n_sample + SLAB
    small_tm = 384 if n_small % 384 == 0 else SLAB
    meta_block = n_sample // SLAB
    zero_conv = jnp.zeros((1, CONV_W - 1, QKV_W), F32)
    zero_pool = jnp.zeros((1, POOL_STATE, POOL_WIDTH), F32)
    zero_ssm = jnp.zeros((1, N_HEADS, HEAD_K, HEAD_V), F32)
    state_ssm = state_ssm.astype(F32)
    prompt_nb = 2 if bp % 2 == 0 else 1
    outs = [[] for _ in range(5)]
    ssm_s = None
    small_ad = (jnp.zeros((n_small, POOL_WIDTH), F32), jnp.zeros((n_small, VAL_W), F32))
    for l in range(depth):
        w = _layer_weights(l, g_pre_mix, w_in, conv_w, a_log, dt_bias, o_norm_g, pool_w, pool_scale,
                           w_branch_pool, w_branch_delta, w_out, g_post_mix, g_pre_ffn, w_up,
                           w_down, g_post_ffn)
        us, qkvs, zs, gatess, gbcs = _in_proj(xs, w, small_tm)
        a_s, d_s, conv_s, pool_s, ssm_s = _mixer(
            (us, qkvs), gbcs, state_conv[l], state_pool[l], state_ssm, w,
            nb=1, nseq=SLAB // dec_seq, seq_len=dec_seq, n_valid=dec_seq, pos0=PAST_LEN,
            n_groups=n_sample // SLAB, n_steps=1, row0=0, shared_state=False,
            layer=l, prev_ssm=ssm_s, prev_ad=small_ad)
        a_small, d_small, conv_m, pool_m, ssm_m = _mixer(
            (us, qkvs), gbcs, zero_conv, zero_pool, zero_ssm, w,
            nb=1, nseq=1, seq_len=SLAB, n_valid=N_META, pos0=0, n_groups=1, n_steps=1,
            row0=meta_block, shared_state=True, prev_ad=(a_s, d_s))
        up, qnp, knp, vp, zp, gatesp, gbcp, conv_p = _in_proj(
            xp, w, 256, conv_prefix=conv_m, n_seq=bp)
        a_p, d_p, pool_p, ssm_p = _mixer(
            (up, qnp, knp, vp), gbcp, None, pool_m, ssm_m, w,
            nb=prompt_nb, nseq=1, seq_len=SLAB, n_valid=SLAB, pos0=N_META,
            n_groups=bp // prompt_nb, n_steps=seq // SLAB, row0=0, shared_state=True)
        xp = _out_ffn(xp, a_p, d_p, zp, gatesp, w, 256, l)
        xs = _out_ffn(xs, a_small, d_small, zs, gatess, w, small_tm, l)
        small_ad = (a_small, d_small)
        for acc, val in zip(outs, (conv_p, ssm_p, pool_p, conv_s, pool_s)):
            acc.append(val)
    y_prompt = xp.reshape(bp, seq, D_MODEL)
    y_sample = xs[:n_sample].reshape(bs, dec_seq, D_MODEL)
    conv_p, ssm_p, pool_p, conv_s, pool_s = (jnp.stack(o) for o in outs)
    return (y_prompt, y_sample, conv_p, ssm_p, pool_p, conv_s, ssm_s, pool_s)
```
